```python
import jax, jax.numpy as jnp
from jax import lax
import numpy as np

D_MODEL = 1024
BATCH = 16
SEQ = 2048
DEPTH = 1

PLE_DIM = 256
CONV_DIM = 512
CONV_WIDTH = 31
POOL_DIM = 512
POOL_WINDOWS = (2, 4, 8, 16)
POOL_GROUPS = len(POOL_WINDOWS)
POOL_GROUP_DIM = POOL_DIM // POOL_GROUPS
POOL_OUT_GROUP_DIM = D_MODEL // POOL_GROUPS
N_BRANCHES = 2
IN_COLS = 2 * CONV_DIM + POOL_DIM + N_BRANCHES * D_MODEL
N_EXPERTS = 16
EXPERT_FF = 1024
CAPACITY_FACTOR = 2
EPS = 1e-6

kernel_name = "hybrid_conv_pool_ec_moe_encoder_block"


def rms_norm(x, g):
    xf = x.astype(jnp.float32)
    y = xf * lax.rsqrt(jnp.mean(xf * xf, axis=-1, keepdims=True) + EPS)
    return (y * g.astype(jnp.float32)).astype(x.dtype)


def layer_norm(x, g, b):
    xf = x.astype(jnp.float32)
    mu = jnp.mean(xf, axis=-1, keepdims=True)
    xc = xf - mu
    var = jnp.mean(xc * xc, axis=-1, keepdims=True)
    y = xc * lax.rsqrt(var + EPS) * g.astype(jnp.float32) + b.astype(jnp.float32)
    return y.astype(x.dtype)


def conformer_conv(a, gate, conv_w, conv_b, ln_g, ln_b, w_out):
    v = a * jax.nn.sigmoid(gate)
    half = CONV_WIDTH // 2
    v = lax.conv_general_dilated(
        v, conv_w[:, None, :].astype(v.dtype), window_strides=(1,),
        padding=[(half, half)], dimension_numbers=("NWC", "WIO", "NWC"),
        feature_group_count=CONV_DIM) + conv_b
    v = jax.nn.silu(layer_norm(v, ln_g, ln_b))
    return jnp.einsum("bsc,cd->bsd", v, w_out)


def multiscale_pool(u, w_pool, scale):
    b, s, _ = u.shape
    ug = u.astype(jnp.float32).reshape(b, s, POOL_GROUPS, POOL_GROUP_DIM)
    c = jnp.concatenate(
        [jnp.zeros((b, 1, POOL_GROUPS, POOL_GROUP_DIM), jnp.float32),
         jnp.cumsum(ug, axis=1)], axis=1)
    t = jnp.arange(s)
    outs = []
    for g, w in enumerate(POOL_WINDOWS):
        lo = jnp.clip(t - w // 2, 0, s - 1)
        hi = jnp.clip(t + (w - w // 2) - 1, 0, s - 1)
        cnt = (hi - lo + 1).astype(jnp.float32)[None, :, None]
        mean = (c[:, hi + 1, g] - c[:, lo, g]) / cnt
        outs.append(mean - ug[:, :, g])
    d = jnp.stack(outs, axis=2).astype(u.dtype)
    y = jnp.einsum("bsgc,gce->bsge", d, w_pool).reshape(b, s, D_MODEL)
    return y * scale


def expert_choice_ffn(h, w_router, w_gate, w_up, w_down):
    b, s, d = h.shape
    cap = max(1, CAPACITY_FACTOR * s // N_EXPERTS)
    aff = jax.nn.softmax(jnp.einsum("bsd,de->bse", h, w_router).astype(jnp.float32), axis=-1)
    top_val, top_idx = lax.top_k(jnp.swapaxes(aff, 1, 2), cap)
    xg = jax.vmap(lambda hb, ib: hb[ib])(h, top_idx)
    hid = jax.nn.silu(jnp.einsum("becd,edf->becf", xg, w_gate)) * \
        jnp.einsum("becd,edf->becf", xg, w_up)
    ye = jnp.einsum("becf,efd->becd", hid, w_down) * top_val[..., None].astype(h.dtype)
    return jax.vmap(
        lambda yb, ib: jnp.zeros((s, d), yb.dtype).at[ib.reshape(-1)].add(yb.reshape(-1, d))
    )(ye, top_idx)


def setup_inputs(seed: int = 0) -> dict:
    key = jax.random.key(seed)
    ks = jax.random.split(key, 24)
    f32 = jnp.float32
    nrm = lambda k, shape, scale: jax.random.normal(k, shape, f32) * scale
    gain = lambda k, shape: 1.0 + 0.02 * jax.random.normal(k, shape, f32)
    L = DEPTH
    return {
        "x": jax.random.normal(ks[0], (BATCH, SEQ, D_MODEL), f32),
        "p": jax.random.normal(ks[1], (DEPTH, BATCH, SEQ, PLE_DIM), f32),
        "norm1_g": gain(ks[2], (L, D_MODEL)),
        "w_in": nrm(ks[3], (L, D_MODEL, IN_COLS), D_MODEL ** -0.5),
        "b_gate": nrm(ks[4], (L, N_BRANCHES * D_MODEL), 0.02),
        "conv_w": nrm(ks[5], (L, CONV_WIDTH, CONV_DIM), CONV_WIDTH ** -0.5),
        "conv_b": nrm(ks[6], (L, CONV_DIM), 0.02),
        "conv_ln_g": gain(ks[7], (L, CONV_DIM)),
        "conv_ln_b": nrm(ks[8], (L, CONV_DIM), 0.02),
        "w_conv_out": nrm(ks[9], (L, CONV_DIM, D_MODEL), CONV_DIM ** -0.5),
        "w_pool": nrm(ks[10], (L, POOL_GROUPS, POOL_GROUP_DIM, POOL_OUT_GROUP_DIM), POOL_GROUP_DIM ** -0.5),
        "pool_scale": gain(ks[11], (L, D_MODEL)),
        "w_out": nrm(ks[12], (L, D_MODEL, D_MODEL), D_MODEL ** -0.5),
        "norm2_g": gain(ks[13], (L, D_MODEL)),
        "w_router": nrm(ks[14], (L, D_MODEL, N_EXPERTS), D_MODEL ** -0.5),
        "w_exp_gate": nrm(ks[15], (L, N_EXPERTS, D_MODEL, EXPERT_FF), D_MODEL ** -0.5),
        "w_exp_up": nrm(ks[16], (L, N_EXPERTS, D_MODEL, EXPERT_FF), D_MODEL ** -0.5),
        "w_exp_down": nrm(ks[17], (L, N_EXPERTS, EXPERT_FF, D_MODEL), EXPERT_FF ** -0.5),
        "norm3_g": gain(ks[18], (L, D_MODEL)),
        "w_ple_gate": nrm(ks[19], (L, D_MODEL, D_MODEL), D_MODEL ** -0.5),
        "b_ple_gate": nrm(ks[20], (L, D_MODEL), 0.02),
        "w_ple": nrm(ks[21], (L, PLE_DIM, D_MODEL), PLE_DIM ** -0.5),
        "ple_norm_g": gain(ks[22], (L, D_MODEL)),
        "final_g": gain(ks[23], (D_MODEL,)),
    }


def reference(x, p, norm1_g, w_in, b_gate, conv_w, conv_b, conv_ln_g, conv_ln_b,
              w_conv_out, w_pool, pool_scale, w_out, norm2_g, w_router,
              w_exp_gate, w_exp_up, w_exp_down, norm3_g, w_ple_gate, b_ple_gate,
              w_ple, ple_norm_g, final_g):
    c1 = CONV_DIM
    c2 = 2 * CONV_DIM
    c3 = c2 + POOL_DIM
    c4 = c3 + D_MODEL
    for i in range(DEPTH):
        h = rms_norm(x, norm1_g[i])
        z = jnp.einsum("bsd,dk->bsk", h, w_in[i])
        gates = jax.nn.sigmoid(z[..., c3:] + b_gate[i])
        y_conv = conformer_conv(z[..., :c1], z[..., c1:c2], conv_w[i], conv_b[i],
                                conv_ln_g[i], conv_ln_b[i], w_conv_out[i])
        y_pool = multiscale_pool(z[..., c2:c3], w_pool[i], pool_scale[i])
        merged = gates[..., :D_MODEL] * y_conv + gates[..., D_MODEL:] * y_pool
        x = x + jnp.einsum("bsd,de->bse", merged, w_out[i])
        x = x + expert_choice_ffn(rms_norm(x, norm2_g[i]), w_router[i],
                                  w_exp_gate[i], w_exp_up[i], w_exp_down[i])
        g = jax.nn.sigmoid(jnp.einsum("bsd,de->bse", rms_norm(x, norm3_g[i]), w_ple_gate[i]) + b_ple_gate[i])
        e = rms_norm(jnp.einsum("bsq,qd->bsd", p[i], w_ple[i]), ple_norm_g[i])
        x = x + g * e
    return rms_norm(x, final_g)
```

```python
import functools

import jax
import jax.numpy as jnp
from jax import lax
from jax.experimental import pallas as pl
from jax.experimental.pallas import tpu as pltpu

F32 = jnp.float32
BF16 = jnp.bfloat16

CONV_DIM = 512
CONV_WIDTH = 31
CONV_HALF = CONV_WIDTH // 2
POOL_DIM = 512
POOL_WINDOWS = (2, 4, 8, 16)
POOL_GROUP_DIM = POOL_DIM // len(POOL_WINDOWS)
N_EXPERTS = 16
CAPACITY_FACTOR = 2
EPS = 1e-6

HALO = 16
VMEM_LIMIT = 56 * 1024 * 1024


def _rms(x, g):
    return x * lax.rsqrt(jnp.mean(x * x, axis=-1, keepdims=True) + EPS) * g


def _sigmoid(x):
    return 1.0 / (1.0 + jnp.exp(-x))


def _inproj_kernel(x_ref, g_ref, w_ref, v_ref, u_ref):
    h = _rms(x_ref[...], g_ref[...]).astype(BF16)
    z = jnp.dot(h, w_ref[...], preferred_element_type=F32)
    v_ref[...] = z[:, :CONV_DIM] * _sigmoid(z[:, CONV_DIM:2 * CONV_DIM])
    u_ref[...] = z[:, 2 * CONV_DIM:]


def _inproj(x2d, g, w, tm):
    t, d = x2d.shape
    n = w.shape[1]
    return pl.pallas_call(
        _inproj_kernel,
        grid=(t // tm,),
        in_specs=[
            pl.BlockSpec((tm, d), lambda i: (i, 0)),
            pl.BlockSpec((1, d), lambda i: (0, 0)),
            pl.BlockSpec((d, n), lambda i: (0, 0)),
        ],
        out_specs=[
            pl.BlockSpec((tm, CONV_DIM), lambda i: (i, 0)),
            pl.BlockSpec((tm, POOL_DIM), lambda i: (i, 0)),
        ],
        out_shape=[
            jax.ShapeDtypeStruct((t, CONV_DIM), F32),
            jax.ShapeDtypeStruct((t, POOL_DIM), F32),
        ],
        compiler_params=pltpu.CompilerParams(
            dimension_semantics=("parallel",), vmem_limit_bytes=VMEM_LIMIT),
        name="inproj",
    )(x2d, g, w)


def _mixer_kernel(x_ref, vc_ref, vp_ref, vn_ref, uc_ref, up_ref, un_ref,
                  n1g_ref, wg_ref, bg_ref, cw_ref, cb_ref, lg_ref, lb_ref, wco_ref,
                  wp_ref, ps_ref, wo_ref, n2g_ref, wrt_ref,
                  x1_ref, h2_ref, afft_ref,
                  vext_ref, uext_ref, conv_ref, *, ts, seq, rc):
    i = pl.program_id(1)
    nt = pl.num_programs(1)
    d = x_ref.shape[-1]

    first = i == 0
    last = i == nt - 1
    vext_ref[0:HALO, :] = jnp.where(first, 0.0, vp_ref[...])
    vext_ref[HALO:HALO + ts, :] = vc_ref[...]
    vext_ref[HALO + ts:2 * HALO + ts, :] = jnp.where(last, 0.0, vn_ref[...])
    uext_ref[0:HALO, :] = jnp.where(first, 0.0, up_ref[...])
    uext_ref[HALO:HALO + ts, :] = uc_ref[...]
    uext_ref[HALO + ts:2 * HALO + ts, :] = jnp.where(last, 0.0, un_ref[...])

    for c in range(ts // rc):
        acc = jnp.zeros((rc, CONV_DIM), F32)
        for k in range(CONV_WIDTH):
            r0 = c * rc + HALO - CONV_HALF + k
            acc = acc + vext_ref[r0:r0 + rc, :] * cw_ref[k:k + 1, :]
        conv_ref[c * rc:(c + 1) * rc, :] = acc + cb_ref[...]

    cv = conv_ref[...]
    mu = jnp.mean(cv, axis=-1, keepdims=True)
    xc = cv - mu
    var = jnp.mean(xc * xc, axis=-1, keepdims=True)
    ln = xc * lax.rsqrt(var + EPS) * lg_ref[...] + lb_ref[...]
    act = (ln * _sigmoid(ln)).astype(BF16)
    y_conv = jnp.dot(act, wco_ref[...], preferred_element_type=F32)

    pos = i * ts + lax.broadcasted_iota(jnp.int32, (ts, 1), 0)
    pooled = []
    for g, w in enumerate(POOL_WINDOWS):
        lanes = slice(g * POOL_GROUP_DIM, (g + 1) * POOL_GROUP_DIM)
        ssum = jnp.zeros((ts, POOL_GROUP_DIM), F32)
        for j in range(-(w // 2), w - w // 2):
            ssum = ssum + uext_ref[HALO + j:HALO + j + ts, lanes]
        lo = jnp.maximum(pos - w // 2, 0)
        hi = jnp.minimum(pos + (w - w // 2) - 1, seq - 1)
        cnt = (hi - lo + 1).astype(F32)
        dgrp = ssum / cnt - uext_ref[HALO:HALO + ts, lanes]
        pooled.append(jnp.dot(dgrp.astype(BF16), wp_ref[g], preferred_element_type=F32))
    y_pool = jnp.concatenate(pooled, axis=-1) * ps_ref[...]

    x = x_ref[...]
    h = _rms(x, n1g_ref[...]).astype(BF16)
    gates = _sigmoid(jnp.dot(h, wg_ref[...], preferred_element_type=F32) + bg_ref[...])
    merged = gates[:, :d] * y_conv + gates[:, d:] * y_pool
    x1 = x + jnp.dot(merged.astype(BF16), wo_ref[...], preferred_element_type=F32)
    x1_ref[...] = x1

    h2 = _rms(x1, n2g_ref[...])
    h2_ref[...] = h2.astype(BF16)
    logits_t = lax.dot_general(wrt_ref[...], h2, (((1,), (1,)), ((), ())),
                               precision=lax.Precision.HIGHEST,
                               preferred_element_type=F32)
    ex = jnp.exp(logits_t - jnp.max(logits_t, axis=0, keepdims=True))
    afft_ref[...] = ex / jnp.sum(ex, axis=0, keepdims=True)


def _mixer(x, v, u, n1g, wg, bg, cw, cb, lg, lb, wco, wp, ps, wo, n2g, wrt, ts, rc):
    b, s, d = x.shape
    nt = s // ts
    hb = ts // HALO
    nhb = s // HALO

    def cur(bi, i):
        return (bi, i, 0)

    def prev(bi, i):
        return (bi, jnp.maximum(i * hb - 1, 0), 0)

    def nxt(bi, i):
        return (bi, jnp.minimum((i + 1) * hb, nhb - 1), 0)

    def full(arr):
        nd = arr.ndim
        return pl.BlockSpec(arr.shape, lambda bi, i: (0,) * nd)

    params = (n1g, wg, bg, cw, cb, lg, lb, wco, wp, ps, wo, n2g, wrt)
    return pl.pallas_call(
        functools.partial(_mixer_kernel, ts=ts, seq=s, rc=rc),
        grid=(b, nt),
        in_specs=[
            pl.BlockSpec((None, ts, d), cur),
            pl.BlockSpec((None, ts, CONV_DIM), cur),
            pl.BlockSpec((None, HALO, CONV_DIM), prev),
            pl.BlockSpec((None, HALO, CONV_DIM), nxt),
            pl.BlockSpec((None, ts, POOL_DIM), cur),
            pl.BlockSpec((None, HALO, POOL_DIM), prev),
            pl.BlockSpec((None, HALO, POOL_DIM), nxt),
        ] + [full(a) for a in params],
        out_specs=[
            pl.BlockSpec((None, ts, d), cur),
            pl.BlockSpec((None, ts, d), cur),
            pl.BlockSpec((None, N_EXPERTS, ts), lambda bi, i: (bi, 0, i)),
        ],
        out_shape=[
            jax.ShapeDtypeStruct((b, s, d), F32),
            jax.ShapeDtypeStruct((b, s, d), BF16),
            jax.ShapeDtypeStruct((b, N_EXPERTS, s), F32),
        ],
        scratch_shapes=[
            pltpu.VMEM((ts + 2 * HALO, CONV_DIM), F32),
            pltpu.VMEM((ts + 2 * HALO, POOL_DIM), F32),
            pltpu.VMEM((ts, CONV_DIM), F32),
        ],
        compiler_params=pltpu.CompilerParams(
            dimension_semantics=("parallel", "parallel"), vmem_limit_bytes=VMEM_LIMIT),
        name="mixer",
    )(x, v, v, v, u, u, u, *params)


def _cumsum_lanes(x):
    n = x.shape[-1]
    lane = lax.broadcasted_iota(jnp.int32, x.shape, x.ndim - 1)
    step = 1
    while step < n:
        x = x + jnp.where(lane >= step, pltpu.roll(x, step, axis=x.ndim - 1), 0)
        step *= 2
    return x


def _topk_kernel(a_ref, code_ref, *, cap):
    a = a_ref[...]
    as_float = lambda word: pltpu.bitcast(word, F32)
    thr = jnp.zeros((a.shape[0], 1), jnp.int32)
    for bit in range(30, -1, -1):
        cand = thr | (1 << bit)
        cnt = jnp.sum((a >= as_float(cand)).astype(jnp.int32), axis=1, keepdims=True)
        thr = jnp.where(cnt >= cap, cand, thr)
    gt = a >= as_float(thr + 1)
    eq = jnp.where(gt, 0, (a >= as_float(thr)).astype(jnp.int32))
    need = cap - jnp.sum(gt.astype(jnp.int32), axis=1, keepdims=True)
    eq_rank = _cumsum_lanes(eq) - eq
    sel = jnp.where(gt, 1, jnp.where(eq_rank < need, eq, 0))
    slot = _cumsum_lanes(sel) - sel
    code_ref[...] = jnp.where(sel > 0, slot, -1)


def _topk(afft, cap):
    b, e, s = afft.shape
    return pl.pallas_call(
        functools.partial(_topk_kernel, cap=cap),
        grid=(b,),
        in_specs=[pl.BlockSpec((None, e, s), lambda bi: (bi, 0, 0))],
        out_specs=pl.BlockSpec((None, e, s), lambda bi: (bi, 0, 0)),
        out_shape=jax.ShapeDtypeStruct((b, e, s), jnp.int32),
        compiler_params=pltpu.CompilerParams(dimension_semantics=("parallel",)),
        name="topk",
    )(afft)


def _experts_kernel(h2_ref, code_ref, aff_ref, wg_ref, wu_ref, wd_ref, out_ref, *, cap):
    e = pl.program_id(1)
    s = h2_ref.shape[0]
    code = code_ref[...]
    hit = code == lax.broadcasted_iota(jnp.int32, (cap, s), 0)
    onehot = jnp.where(hit, 1.0, 0.0).astype(BF16)
    xg = jnp.dot(onehot, h2_ref[...], preferred_element_type=F32).astype(BF16)
    hg = jnp.dot(xg, wg_ref[...], preferred_element_type=F32)
    hu = jnp.dot(xg, wu_ref[...], preferred_element_type=F32)
    hid = (hg * _sigmoid(hg) * hu).astype(BF16)
    ye = jnp.dot(hid, wd_ref[...], preferred_element_type=F32)
    weighted = jnp.where(hit, aff_ref[...], 0.0)
    contrib = jnp.dot(weighted.T.astype(BF16), ye.astype(BF16), preferred_element_type=F32)

    @pl.when(e == 0)
    def _():
        out_ref[...] = contrib

    @pl.when(e > 0)
    def _():
        out_ref[...] += contrib


def _experts(h2, code, afft, wg, wu, wd, cap):
    b, s, d = h2.shape
    ne, _, f = wg.shape
    code4 = code.reshape(b, ne, 1, s)
    aff4 = afft.reshape(b, ne, 1, s)
    return pl.pallas_call(
        functools.partial(_experts_kernel, cap=cap),
        grid=(b, ne),
        in_specs=[
            pl.BlockSpec((None, s, d), lambda bi, e: (bi, 0, 0)),
            pl.BlockSpec((None, None, 1, s), lambda bi, e: (bi, e, 0, 0)),
            pl.BlockSpec((None, None, 1, s), lambda bi, e: (bi, e, 0, 0)),
            pl.BlockSpec((None, d, f), lambda bi, e: (e, 0, 0)),
            pl.BlockSpec((None, d, f), lambda bi, e: (e, 0, 0)),
            pl.BlockSpec((None, f, d), lambda bi, e: (e, 0, 0)),
        ],
        out_specs=pl.BlockSpec((None, s, d), lambda bi, e: (bi, 0, 0)),
        out_shape=jax.ShapeDtypeStruct((b, s, d), F32),
        compiler_params=pltpu.CompilerParams(
            dimension_semantics=("parallel", "arbitrary"), vmem_limit_bytes=VMEM_LIMIT),
        name="experts",
    )(h2, code4, aff4, wg, wu, wd)


def _final_kernel(x1_ref, moe_ref, p_ref, n3g_ref, wpg_ref, bpg_ref, wple_ref, pg_ref, fg_ref, o_ref,
                  *, last_layer):
    x2 = x1_ref[...] + moe_ref[...]
    hn = _rms(x2, n3g_ref[...]).astype(BF16)
    gate = _sigmoid(jnp.dot(hn, wpg_ref[...], preferred_element_type=F32) + bpg_ref[...])
    emb = jnp.dot(p_ref[...].astype(BF16), wple_ref[...], preferred_element_type=F32)
    x3 = x2 + gate * _rms(emb, pg_ref[...])
    o_ref[...] = _rms(x3, fg_ref[...]) if last_layer else x3


def _final(x1, moe, p2d, n3g, wpg, bpg, wple, pg, fg, tm, last_layer):
    t, d = x1.shape
    q = p2d.shape[1]

    def full(arr):
        nd = arr.ndim
        return pl.BlockSpec(arr.shape, lambda i: (0,) * nd)

    params = (n3g, wpg, bpg, wple, pg, fg)
    return pl.pallas_call(
        functools.partial(_final_kernel, last_layer=last_layer),
        grid=(t // tm,),
        in_specs=[
            pl.BlockSpec((tm, d), lambda i: (i, 0)),
            pl.BlockSpec((tm, d), lambda i: (i, 0)),
            pl.BlockSpec((tm, q), lambda i: (i, 0)),
        ] + [full(a) for a in params],
        out_specs=pl.BlockSpec((tm, d), lambda i: (i, 0)),
        out_shape=jax.ShapeDtypeStruct((t, d), F32),
        compiler_params=pltpu.CompilerParams(
            dimension_semantics=("parallel",), vmem_limit_bytes=VMEM_LIMIT),
        name="final",
    )(x1, moe, p2d, *params)


def kernel(x, p, norm1_g, w_in, b_gate, conv_w, conv_b, conv_ln_g, conv_ln_b, w_conv_out, w_pool,
           pool_scale, w_out, norm2_g, w_router, w_exp_gate, w_exp_up, w_exp_down, norm3_g,
           w_ple_gate, b_ple_gate, w_ple, ple_norm_g, final_g):
    b, s, d = x.shape
    depth = w_in.shape[0]
    cap = max(1, CAPACITY_FACTOR * s // N_EXPERTS)
    c3 = 2 * CONV_DIM + POOL_DIM
    row = lambda a: a.reshape(1, -1)

    for l in range(depth):
        w_in_b = w_in[l].astype(BF16)
        v, u = _inproj(x.reshape(b * s, d), row(norm1_g[l]), w_in_b[:, :c3], tm=512)
        x1, h2, afft = _mixer(
            x, v.reshape(b, s, CONV_DIM), u.reshape(b, s, POOL_DIM),
            row(norm1_g[l]), w_in_b[:, c3:], row(b_gate[l]), conv_w[l], row(conv_b[l]),
            row(conv_ln_g[l]), row(conv_ln_b[l]), w_conv_out[l].astype(BF16),
            w_pool[l].astype(BF16), row(pool_scale[l]), w_out[l].astype(BF16),
            row(norm2_g[l]), w_router[l].T, ts=512, rc=64)
        code = _topk(afft, cap)
        moe = _experts(h2, code, afft, w_exp_gate[l].astype(BF16), w_exp_up[l].astype(BF16),
                       w_exp_down[l].astype(BF16), cap)
        x = _final(x1.reshape(b * s, d), moe.reshape(b * s, d), p[l].reshape(b * s, -1),
                   row(norm3_g[l]), w_ple_gate[l].astype(BF16), row(b_ple_gate[l]),
                   w_ple[l].astype(BF16), row(ple_norm_g[l]),
                   row(final_g), tm=512, last_layer=l == depth - 1).reshape(b, s, d)
    return x
```

```python
import functools

import jax
import jax.numpy as jnp
from jax import lax
from jax.experimental import pallas as pl
from jax.experimental.pallas import tpu as pltpu

F32 = jnp.float32
BF16 = jnp.bfloat16

CONV_DIM = 512
CONV_WIDTH = 31
CONV_HALF = CONV_WIDTH // 2
POOL_DIM = 512
POOL_WINDOWS = (2, 4, 8, 16)
POOL_GROUP_DIM = POOL_DIM // len(POOL_WINDOWS)
N_EXPERTS = 16
CAPACITY_FACTOR = 2
EPS = 1e-6

SUBLANES = 8
HALO = 16
VMEM_LIMIT = 56 * 1024 * 1024
EXPERTS_VMEM_LIMIT = 60 * 1024 * 1024


def _rms(x, g):
    return x * lax.rsqrt(jnp.mean(x * x, axis=-1, keepdims=True) + EPS) * g


def _sigmoid(x):
    return 1.0 / (1.0 + jnp.exp(-x))


def _inproj_kernel(x_ref, g_ref, w_ref, v_ref, u_ref):
    h = _rms(x_ref[...], g_ref[...]).astype(BF16)
    z = jnp.dot(h, w_ref[...], preferred_element_type=F32)
    v_ref[...] = z[:, :CONV_DIM] * _sigmoid(z[:, CONV_DIM:2 * CONV_DIM])
    u_ref[...] = z[:, 2 * CONV_DIM:]


def _inproj(x2d, g, w, tm):
    t, d = x2d.shape
    n = w.shape[1]
    return pl.pallas_call(
        _inproj_kernel,
        grid=(t // tm,),
        in_specs=[
            pl.BlockSpec((tm, d), lambda i: (i, 0)),
            pl.BlockSpec((1, d), lambda i: (0, 0)),
            pl.BlockSpec((d, n), lambda i: (0, 0)),
        ],
        out_specs=[
            pl.BlockSpec((tm, CONV_DIM), lambda i: (i, 0)),
            pl.BlockSpec((tm, POOL_DIM), lambda i: (i, 0)),
        ],
        out_shape=[
            jax.ShapeDtypeStruct((t, CONV_DIM), F32),
            jax.ShapeDtypeStruct((t, POOL_DIM), F32),
        ],
        compiler_params=pltpu.CompilerParams(
            dimension_semantics=("parallel",), vmem_limit_bytes=VMEM_LIMIT),
        name="inproj",
    )(x2d, g, w)


def _mixer_kernel(x_ref, vc_ref, vp_ref, vn_ref, uc_ref, up_ref, un_ref,
                  n1g_ref, wg_ref, bg_ref, cw_ref, cb_ref, lg_ref, lb_ref, wco_ref,
                  wp_ref, ps_ref, wo_ref, n2g_ref, wrt_ref,
                  x1_ref, h2_ref, afft_ref,
                  vext_ref, uext_ref, vsh_ref, conv_ref, *, ts, seq, rc):
    i = pl.program_id(1)
    nt = pl.num_programs(1)
    d = x_ref.shape[-1]

    first = i == 0
    last = i == nt - 1
    vext_ref[0:HALO, :] = jnp.where(first, 0.0, vp_ref[...])
    vext_ref[HALO:HALO + ts, :] = vc_ref[...]
    vext_ref[HALO + ts:2 * HALO + ts, :] = jnp.where(last, 0.0, vn_ref[...])
    uext_ref[0:HALO, :] = jnp.where(first, 0.0, up_ref[...])
    uext_ref[HALO:HALO + ts, :] = uc_ref[...]
    uext_ref[HALO + ts:2 * HALO + ts, :] = jnp.where(last, 0.0, un_ref[...])

    nsh = vsh_ref.shape[1]
    for ph in range(SUBLANES):
        vsh_ref[ph] = vext_ref[ph:ph + nsh, :]
    for c in range(ts // rc):
        acc = jnp.zeros((rc, CONV_DIM), F32)
        for k in range(CONV_WIDTH):
            off = HALO - CONV_HALF + k
            ph = off % SUBLANES
            r0 = c * rc + off - ph
            acc = acc + vsh_ref[ph, r0:r0 + rc, :] * cw_ref[k:k + 1, :]
        conv_ref[c * rc:(c + 1) * rc, :] = acc + cb_ref[...]

    cv = conv_ref[...]
    mu = jnp.mean(cv, axis=-1, keepdims=True)
    xc = cv - mu
    var = jnp.mean(xc * xc, axis=-1, keepdims=True)
    ln = xc * lax.rsqrt(var + EPS) * lg_ref[...] + lb_ref[...]
    act = (ln * _sigmoid(ln)).astype(BF16)
    y_conv = jnp.dot(act, wco_ref[...], preferred_element_type=F32)

    pos = i * ts + lax.broadcasted_iota(jnp.int32, (ts, 1), 0)
    pooled = []
    for g, w in enumerate(POOL_WINDOWS):
        lanes = slice(g * POOL_GROUP_DIM, (g + 1) * POOL_GROUP_DIM)
        ssum = jnp.zeros((ts, POOL_GROUP_DIM), F32)
        for j in range(-(w // 2), w - w // 2):
            ssum = ssum + uext_ref[HALO + j:HALO + j + ts, lanes]
        lo = jnp.maximum(pos - w // 2, 0)
        hi = jnp.minimum(pos + (w - w // 2) - 1, seq - 1)
        cnt = (hi - lo + 1).astype(F32)
        dgrp = ssum / cnt - uext_ref[HALO:HALO + ts, lanes]
        pooled.append(jnp.dot(dgrp.astype(BF16), wp_ref[g], preferred_element_type=F32))
    y_pool = jnp.concatenate(pooled, axis=-1) * ps_ref[...]

    x = x_ref[...]
    h = _rms(x, n1g_ref[...]).astype(BF16)
    gates = _sigmoid(jnp.dot(h, wg_ref[...], preferred_element_type=F32) + bg_ref[...])
    merged = gates[:, :d] * y_conv + gates[:, d:] * y_pool
    x1 = x + jnp.dot(merged.astype(BF16), wo_ref[...], preferred_element_type=F32)
    x1_ref[...] = x1

    h2 = _rms(x1, n2g_ref[...])
    h2_ref[...] = h2
    logits_t = lax.dot_general(wrt_ref[...], h2, (((1,), (1,)), ((), ())),
                               precision=lax.Precision.HIGHEST,
                               preferred_element_type=F32)
    ex = jnp.exp(logits_t - jnp.max(logits_t, axis=0, keepdims=True))
    afft_ref[...] = ex / jnp.sum(ex, axis=0, keepdims=True)


def _mixer(x, v, u, n1g, wg, bg, cw, cb, lg, lb, wco, wp, ps, wo, n2g, wrt, ts, rc):
    b, s, d = x.shape
    nt = s // ts
    hb = ts // HALO
    nhb = s // HALO

    def cur(bi, i):
        return (bi, i, 0)

    def prev(bi, i):
        return (bi, jnp.maximum(i * hb - 1, 0), 0)

    def nxt(bi, i):
        return (bi, jnp.minimum((i + 1) * hb, nhb - 1), 0)

    def full(arr):
        nd = arr.ndim
        return pl.BlockSpec(arr.shape, lambda bi, i: (0,) * nd)

    params = (n1g, wg, bg, cw, cb, lg, lb, wco, wp, ps, wo, n2g, wrt)
    return pl.pallas_call(
        functools.partial(_mixer_kernel, ts=ts, seq=s, rc=rc),
        grid=(b, nt),
        in_specs=[
            pl.BlockSpec((None, ts, d), cur),
            pl.BlockSpec((None, ts, CONV_DIM), cur),
            pl.BlockSpec((None, HALO, CONV_DIM), prev),
            pl.BlockSpec((None, HALO, CONV_DIM), nxt),
            pl.BlockSpec((None, ts, POOL_DIM), cur),
            pl.BlockSpec((None, HALO, POOL_DIM), prev),
            pl.BlockSpec((None, HALO, POOL_DIM), nxt),
        ] + [full(a) for a in params],
        out_specs=[
            pl.BlockSpec((None, ts, d), cur),
            pl.BlockSpec((None, ts, d), cur),
            pl.BlockSpec((None, N_EXPERTS, ts), lambda bi, i: (bi, 0, i)),
        ],
        out_shape=[
            jax.ShapeDtypeStruct((b, s, d), F32),
            jax.ShapeDtypeStruct((b, s, d), F32),
            jax.ShapeDtypeStruct((b, N_EXPERTS, s), F32),
        ],
        scratch_shapes=[
            pltpu.VMEM((ts + 2 * HALO, CONV_DIM), F32),
            pltpu.VMEM((ts + 2 * HALO, POOL_DIM), F32),
            pltpu.VMEM((SUBLANES, ts + 2 * HALO - SUBLANES, CONV_DIM), F32),
            pltpu.VMEM((ts, CONV_DIM), F32),
        ],
        compiler_params=pltpu.CompilerParams(
            dimension_semantics=("parallel", "parallel"), vmem_limit_bytes=VMEM_LIMIT),
        name="mixer",
    )(x, v, v, v, u, u, u, *params)


def _cumsum_lanes(x):
    n = x.shape[-1]
    lane = lax.broadcasted_iota(jnp.int32, x.shape, x.ndim - 1)
    step = 1
    while step < n:
        x = x + jnp.where(lane >= step, pltpu.roll(x, step, axis=x.ndim - 1), 0)
        step *= 2
    return x


TOKEN_RADIX = 64


def _topk_kernel(a_ref, idx_ref, w_ref, *, cap):
    a = a_ref[...]
    as_float = lambda word: pltpu.bitcast(word, F32)
    thr = jnp.zeros((a.shape[0], 1), jnp.int32)
    for bit in range(30, -1, -1):
        cand = thr | (1 << bit)
        cnt = jnp.sum((a >= as_float(cand)).astype(jnp.int32), axis=1, keepdims=True)
        thr = jnp.where(cnt >= cap, cand, thr)
    gt = a >= as_float(thr + 1)
    eq = jnp.where(gt, 0, (a >= as_float(thr)).astype(jnp.int32))
    need = cap - jnp.sum(gt.astype(jnp.int32), axis=1, keepdims=True)
    eq_rank = _cumsum_lanes(eq) - eq
    sel = jnp.where(gt, 1, jnp.where(eq_rank < need, eq, 0))
    slot = _cumsum_lanes(sel) - sel
    code = jnp.where(sel > 0, slot, -1)

    n_exp, s = a.shape
    tok = lax.broadcasted_iota(jnp.int32, (1, s), 1)
    tok_hi = (tok // TOKEN_RADIX).astype(F32)
    tok_lo = (tok % TOKEN_RADIX).astype(F32)
    a_hi = a.astype(BF16).astype(F32)
    a_mid = (a - a_hi).astype(BF16).astype(F32)
    a_lo = ((a - a_hi) - a_mid).astype(BF16).astype(F32)
    slot_iota = lax.broadcasted_iota(jnp.int32, (cap, s), 0)
    r = lax.broadcasted_iota(jnp.int32, (2 * SUBLANES, s), 0)
    for e in range(n_exp):
        onehot = jnp.where(code[e:e + 1, :] == slot_iota, 1.0, 0.0).astype(BF16)
        rows = jnp.where(r == 0, tok_hi, jnp.where(r == 1, tok_lo, jnp.where(
            r == 2, a_hi[e:e + 1], jnp.where(r == 3, a_mid[e:e + 1], jnp.where(
                r == 4, a_lo[e:e + 1], 0.0))))).astype(BF16)
        got = lax.dot_general(rows, onehot, (((1,), (1,)), ((), ())),
                              preferred_element_type=F32)
        idx_ref[e:e + 1, :] = (got[0:1] * TOKEN_RADIX + got[1:2]).astype(jnp.int32)
        w_ref[e:e + 1, :] = got[2:3] + got[3:4] + got[4:5]


def _topk(afft, cap):
    b, e, s = afft.shape
    return pl.pallas_call(
        functools.partial(_topk_kernel, cap=cap),
        grid=(b,),
        in_specs=[pl.BlockSpec((None, e, s), lambda bi: (bi, 0, 0))],
        out_specs=[pl.BlockSpec((None, e, cap), lambda bi: (bi, 0, 0)),
                   pl.BlockSpec((None, e, cap), lambda bi: (bi, 0, 0))],
        out_shape=[jax.ShapeDtypeStruct((b, e, cap), jnp.int32),
                   jax.ShapeDtypeStruct((b, e, cap), F32)],
        compiler_params=pltpu.CompilerParams(dimension_semantics=("parallel",)),
        name="topk",
    )(afft)


SCATTER_GROUP = 8


def _experts_kernel(idx_ref, w_ref, h2_ref, wg_ref, wu_ref, wd_ref, out_ref,
                    xg_a, xg_b, ye_a, ye_b, *, cap, n_exp, n_real):
    g = pl.program_id(0)

    def gather(pair, dst):
        base = pair * cap
        for c in range(cap):
            dst[c:c + 1, :] = h2_ref[pl.ds(idx_ref[base + c], 1), :]

    @pl.when(g == 0)
    def _():
        ye_b[...] = jnp.zeros_like(ye_b)
        gather(0, xg_a)

    @pl.when((g == 0) | ((g - 1) % n_exp == 0))
    def _():
        out_ref[...] = jnp.zeros_like(out_ref)

    def step(xg_cur, xg_nxt, ye_cur, ye_prv):
        gather(jnp.minimum(g + 1, n_real - 1), xg_nxt)
        xg = xg_cur[...].astype(BF16)
        hg = jnp.dot(xg, wg_ref[...], preferred_element_type=F32)
        hu = jnp.dot(xg, wu_ref[...], preferred_element_type=F32)
        hid = (hg * _sigmoid(hg) * hu).astype(BF16)
        ye_cur[...] = jnp.dot(hid, wd_ref[...], preferred_element_type=F32)
        base = jnp.maximum(g - 1, 0) * cap
        for c0 in range(0, cap, SCATTER_GROUP):
            toks = [idx_ref[base + c0 + i] for i in range(SCATTER_GROUP)]
            rows = [out_ref[pl.ds(toks[i], 1), :] + ye_prv[c0 + i:c0 + i + 1, :] * w_ref[base + c0 + i]
                    for i in range(SCATTER_GROUP)]
            for i in range(SCATTER_GROUP):
                out_ref[pl.ds(toks[i], 1), :] = rows[i]

    @pl.when(g % 2 == 0)
    def _():
        step(xg_a, xg_b, ye_a, ye_b)

    @pl.when(g % 2 == 1)
    def _():
        step(xg_b, xg_a, ye_b, ye_a)


def _experts(h2, idx, w, wg, wu, wd, cap):
    b, s, d = h2.shape
    ne, _, f = wg.shape
    n_real = b * ne
    last = n_real - 1
    seq_of = lambda pair: pair // ne
    weights = lambda g, idx_, w_: (jnp.minimum(g, last) % ne, 0, 0)
    return pl.pallas_call(
        functools.partial(_experts_kernel, cap=cap, n_exp=ne, n_real=n_real),
        grid_spec=pltpu.PrefetchScalarGridSpec(
            num_scalar_prefetch=2,
            grid=(n_real + 1,),
            in_specs=[
                pl.BlockSpec((None, s, d), lambda g, idx_, w_: (seq_of(jnp.minimum(g + 1, last)), 0, 0)),
                pl.BlockSpec((None, d, f), weights),
                pl.BlockSpec((None, d, f), weights),
                pl.BlockSpec((None, f, d), weights),
            ],
            out_specs=pl.BlockSpec((None, s, d), lambda g, idx_, w_: (seq_of(jnp.maximum(g - 1, 0)), 0, 0)),
            scratch_shapes=[pltpu.VMEM((cap, d), F32)] * 4,
        ),
        out_shape=jax.ShapeDtypeStruct((b, s, d), F32),
        compiler_params=pltpu.CompilerParams(
            dimension_semantics=("arbitrary",), vmem_limit_bytes=EXPERTS_VMEM_LIMIT),
        name="experts",
    )(idx.reshape(-1), w.reshape(-1), h2, wg, wu, wd)


def _final_kernel(x1_ref, moe_ref, p_ref, n3g_ref, wpg_ref, bpg_ref, wple_ref, pg_ref, fg_ref, o_ref,
                  *, last_layer):
    x2 = x1_ref[...] + moe_ref[...]
    hn = _rms(x2, n3g_ref[...]).astype(BF16)
    gate = _sigmoid(jnp.dot(hn, wpg_ref[...], preferred_element_type=F32) + bpg_ref[...])
    emb = jnp.dot(p_ref[...].astype(BF16), wple_ref[...], preferred_element_type=F32)
    x3 = x2 + gate * _rms(emb, pg_ref[...])
    o_ref[...] = _rms(x3, fg_ref[...]) if last_layer else x3


def _final(x1, moe, p2d, n3g, wpg, bpg, wple, pg, fg, tm, last_layer):
    t, d = x1.shape
    q = p2d.shape[1]

    def full(arr):
        nd = arr.ndim
        return pl.BlockSpec(arr.shape, lambda i: (0,) * nd)

    params = (n3g, wpg, bpg, wple, pg, fg)
    return pl.pallas_call(
        functools.partial(_final_kernel, last_layer=last_layer),
        grid=(t // tm,),
        in_specs=[
            pl.BlockSpec((tm, d), lambda i: (i, 0)),
            pl.BlockSpec((tm, d), lambda i: (i, 0)),
            pl.BlockSpec((tm, q), lambda i: (i, 0)),
        ] + [full(a) for a in params],
        out_specs=pl.BlockSpec((tm, d), lambda i: (i, 0)),
        out_shape=jax.ShapeDtypeStruct((t, d), F32),
        compiler_params=pltpu.CompilerParams(
            dimension_semantics=("parallel",), vmem_limit_bytes=VMEM_LIMIT),
        name="final",
    )(x1, moe, p2d, *params)


def kernel(x, p, norm1_g, w_in, b_gate, conv_w, conv_b, conv_ln_g, conv_ln_b, w_conv_out, w_pool,
           pool_scale, w_out, norm2_g, w_router, w_exp_gate, w_exp_up, w_exp_down, norm3_g,
           w_ple_gate, b_ple_gate, w_ple, ple_norm_g, final_g):
    b, s, d = x.shape
    depth = w_in.shape[0]
    cap = max(1, CAPACITY_FACTOR * s // N_EXPERTS)
    c3 = 2 * CONV_DIM + POOL_DIM
    row = lambda a: a.reshape(1, -1)

    for l in range(depth):
        w_in_b = w_in[l].astype(BF16)
        v, u = _inproj(x.reshape(b * s, d), row(norm1_g[l]), w_in_b[:, :c3], tm=512)
        x1, h2, afft = _mixer(
            x, v.reshape(b, s, CONV_DIM), u.reshape(b, s, POOL_DIM),
            row(norm1_g[l]), w_in_b[:, c3:], row(b_gate[l]), conv_w[l], row(conv_b[l]),
            row(conv_ln_g[l]), row(conv_ln_b[l]), w_conv_out[l].astype(BF16),
            w_pool[l].astype(BF16), row(pool_scale[l]), w_out[l].astype(BF16),
            row(norm2_g[l]), w_router[l].T, ts=512, rc=64)
        idx, wts = _topk(afft, cap)
        moe = _experts(h2, idx, wts, w_exp_gate[l].astype(BF16), w_exp_up[l].astype(BF16),
                       w_exp_down[l].astype(BF16), cap)
        x = _final(x1.reshape(b * s, d), moe.reshape(b * s, d), p[l].reshape(b * s, -1),
                   row(norm3_g[l]), w_ple_gate[l].astype(BF16), row(b_ple_gate[l]),
                   w_ple[l].astype(BF16), row(ple_norm_g[l]),
                   row(final_g), tm=512, last_layer=l == depth - 1).reshape(b, s, d)
    return x
```

```python
import functools

import jax
import jax.numpy as jnp
from jax import lax
from jax.experimental import pallas as pl
from jax.experimental.pallas import tpu as pltpu

F32 = jnp.float32
BF16 = jnp.bfloat16

CONV_DIM = 512
CONV_WIDTH = 31
CONV_HALF = CONV_WIDTH // 2
POOL_DIM = 512
POOL_WINDOWS = (2, 4, 8, 16)
POOL_GROUP_DIM = POOL_DIM // len(POOL_WINDOWS)
N_EXPERTS = 16
CAPACITY_FACTOR = 2
EPS = 1e-6

SUBLANES = 8
HALO = 16
VMEM_LIMIT = 56 * 1024 * 1024
EXPERTS_VMEM_LIMIT = 60 * 1024 * 1024


def _rms(x, g):
    return x * lax.rsqrt(jnp.mean(x * x, axis=-1, keepdims=True) + EPS) * g


def _sigmoid(x):
    return 1.0 / (1.0 + jnp.exp(-x))


def _mixer_kernel(x_ref, xp_ref, xn_ref,
                  n1g_ref, win_ref, bg_ref, cw_ref, cb_ref, lg_ref, lb_ref, wco_ref,
                  wp_ref, ps_ref, wo_ref, n2g_ref, wrt_ref,
                  x1_ref, h2_ref, afft_ref,
                  hext_ref, *chain_scratch, ts, seq, rc, n_chains):
    i = pl.program_id(1)
    nt = pl.num_programs(1)
    d = x_ref.shape[-1]
    c2 = 2 * CONV_DIM
    c3 = c2 + POOL_DIM
    tc = ts // n_chains
    ext = tc + 2 * HALO

    hext_ref[0:HALO, :] = _rms(xp_ref[...], n1g_ref[...]).astype(BF16)
    hext_ref[HALO:HALO + ts, :] = _rms(x_ref[...], n1g_ref[...]).astype(BF16)
    hext_ref[HALO + ts:2 * HALO + ts, :] = _rms(xn_ref[...], n1g_ref[...]).astype(BF16)

    def scratch(j):
        per_chain = len(chain_scratch) // n_chains
        return chain_scratch[j * per_chain:(j + 1) * per_chain]

    def in_proj(j):
        vext_ref, uext_ref, _, _ = scratch(j)
        zc = jnp.dot(hext_ref[j * tc:j * tc + ext, :], win_ref[:, :c3], preferred_element_type=F32)
        vext_ref[...] = zc[:, :CONV_DIM] * _sigmoid(zc[:, CONV_DIM:c2])
        uext_ref[...] = zc[:, c2:]
        if j == 0:
            for ref in (vext_ref, uext_ref):
                ref[0:HALO, :] = jnp.where(i == 0, 0.0, ref[0:HALO, :])
        if j == n_chains - 1:
            for ref in (vext_ref, uext_ref):
                ref[HALO + tc:ext, :] = jnp.where(i == nt - 1, 0.0, ref[HALO + tc:ext, :])

    def gate_logits(j):
        h = hext_ref[HALO + j * tc:HALO + (j + 1) * tc, :]
        return jnp.dot(h, win_ref[:, c3:], preferred_element_type=F32) + bg_ref[...]

    def conv_branch(j):
        vext_ref, _, vsh_ref, conv_ref = scratch(j)
        nsh = vsh_ref.shape[1]
        for ph in range(SUBLANES):
            vsh_ref[ph] = vext_ref[ph:ph + nsh, :]
        for c in range(tc // rc):
            acc = jnp.zeros((rc, CONV_DIM), F32)
            for k in range(CONV_WIDTH):
                off = HALO - CONV_HALF + k
                ph = off % SUBLANES
                r0 = c * rc + off - ph
                acc = acc + vsh_ref[ph, r0:r0 + rc, :] * cw_ref[k:k + 1, :]
            conv_ref[c * rc:(c + 1) * rc, :] = acc + cb_ref[...]
        cv = conv_ref[...]
        mu = jnp.mean(cv, axis=-1, keepdims=True)
        xc = cv - mu
        var = jnp.mean(xc * xc, axis=-1, keepdims=True)
        ln = xc * lax.rsqrt(var + EPS) * lg_ref[...] + lb_ref[...]
        return (ln * _sigmoid(ln)).astype(BF16)

    def merge(j, act, gate_pre):
        _, uext_ref, _, _ = scratch(j)
        rows = slice(j * tc, (j + 1) * tc)
        y_conv = jnp.dot(act, wco_ref[...], preferred_element_type=F32)

        pos = i * ts + j * tc + lax.broadcasted_iota(jnp.int32, (tc, 1), 0)
        pooled = []
        for g, w in enumerate(POOL_WINDOWS):
            lanes = slice(g * POOL_GROUP_DIM, (g + 1) * POOL_GROUP_DIM)
            ue = uext_ref[:, lanes]
            run, width = ue, 1
            while width < w:
                run = run + pltpu.roll(run, width, axis=0)
                width *= 2
            ahead = (w - w // 2) - 1
            if ahead:
                run = pltpu.roll(run, ext - ahead, axis=0)
            lo = jnp.maximum(pos - w // 2, 0)
            hi = jnp.minimum(pos + ahead, seq - 1)
            cnt = (hi - lo + 1).astype(F32)
            dgrp = run[HALO:HALO + tc] / cnt - ue[HALO:HALO + tc]
            pooled.append(jnp.dot(dgrp.astype(BF16), wp_ref[g], preferred_element_type=F32))
        y_pool = jnp.concatenate(pooled, axis=-1) * ps_ref[...]

        gates = _sigmoid(gate_pre)
        merged = gates[:, :d] * y_conv + gates[:, d:] * y_pool
        x1 = x_ref[rows, :] + jnp.dot(merged.astype(BF16), wo_ref[...], preferred_element_type=F32)
        x1_ref[rows, :] = x1

        h2 = _rms(x1, n2g_ref[...])
        h2_ref[rows, :] = h2
        h2_hi = h2.astype(BF16)
        h2_lo = (h2 - h2_hi.astype(F32)).astype(BF16)
        logits_t = lax.dot_general(wrt_ref[...], jnp.concatenate([h2_hi, h2_lo, h2_hi], axis=-1),
                                   (((1,), (1,)), ((), ())), preferred_element_type=F32)
        ex = jnp.exp(logits_t - jnp.max(logits_t, axis=0, keepdims=True))
        afft_ref[:, rows] = ex / jnp.sum(ex, axis=0, keepdims=True)

    in_proj(0)
    pending = None
    for j in range(n_chains):
        if j + 1 < n_chains:
            in_proj(j + 1)
        gate_pre = gate_logits(j)
        act = conv_branch(j)
        if pending is not None:
            merge(*pending)
        pending = (j, act, gate_pre)
    merge(*pending)


def _mixer(x, n1g, win, bg, cw, cb, lg, lb, wco, wp, ps, wo, n2g, wrt, ts, rc, n_chains):
    b, s, d = x.shape
    nt = s // ts
    hb = ts // HALO
    nhb = s // HALO
    tc = ts // n_chains

    def cur(bi, i):
        return (bi, i, 0)

    def prev(bi, i):
        return (bi, jnp.maximum(i * hb - 1, 0), 0)

    def nxt(bi, i):
        return (bi, jnp.minimum((i + 1) * hb, nhb - 1), 0)

    def full(arr):
        nd = arr.ndim
        return pl.BlockSpec(arr.shape, lambda bi, i: (0,) * nd, pipeline_mode=pl.Buffered(1))

    params = (n1g, win, bg, cw, cb, lg, lb, wco, wp, ps, wo, n2g, wrt)
    chain_scratch = [
        pltpu.VMEM((tc + 2 * HALO, CONV_DIM), F32),
        pltpu.VMEM((tc + 2 * HALO, POOL_DIM), F32),
        pltpu.VMEM((SUBLANES, tc + 2 * HALO - SUBLANES, CONV_DIM), F32),
        pltpu.VMEM((tc, CONV_DIM), F32),
    ]
    return pl.pallas_call(
        functools.partial(_mixer_kernel, ts=ts, seq=s, rc=rc, n_chains=n_chains),
        grid=(b, nt),
        in_specs=[
            pl.BlockSpec((None, ts, d), cur),
            pl.BlockSpec((None, HALO, d), prev),
            pl.BlockSpec((None, HALO, d), nxt),
        ] + [full(a) for a in params],
        out_specs=[
            pl.BlockSpec((None, ts, d), cur),
            pl.BlockSpec((None, ts, d), cur),
            pl.BlockSpec((None, N_EXPERTS, ts), lambda bi, i: (bi, 0, i)),
        ],
        out_shape=[
            jax.ShapeDtypeStruct((b, s, d), F32),
            jax.ShapeDtypeStruct((b, s, d), F32),
            jax.ShapeDtypeStruct((b, N_EXPERTS, s), F32),
        ],
        scratch_shapes=[pltpu.VMEM((ts + 2 * HALO, d), BF16)] + chain_scratch * n_chains,
        compiler_params=pltpu.CompilerParams(
            dimension_semantics=("parallel", "parallel"), vmem_limit_bytes=VMEM_LIMIT),
        name="mixer",
    )(x, x, x, *params)


def _cumsum_lanes(x):
    n = x.shape[-1]
    lane = lax.broadcasted_iota(jnp.int32, x.shape, x.ndim - 1)
    step = 1
    while step < n:
        x = x + jnp.where(lane >= step, pltpu.roll(x, step, axis=x.ndim - 1), 0)
        step *= 2
    return x


NO_TOKEN = 1 << 20


def _topk_kernel(a_ref, idx_ref, w_ref, *, cap):
    a = a_ref[...]
    as_float = lambda word: pltpu.bitcast(word, F32)
    thr = jnp.zeros((a.shape[0], 1), jnp.int32)
    for bit in range(30, -1, -1):
        cand = thr | (1 << bit)
        cnt = jnp.sum((a >= as_float(cand)).astype(jnp.int32), axis=1, keepdims=True)
        thr = jnp.where(cnt >= cap, cand, thr)
    gt = a >= as_float(thr + 1)
    eq = jnp.where(gt, 0, (a >= as_float(thr)).astype(jnp.int32))
    need = cap - jnp.sum(gt.astype(jnp.int32), axis=1, keepdims=True)
    eq_rank = _cumsum_lanes(eq) - eq
    sel = jnp.where(gt, 1, jnp.where(eq_rank < need, eq, 0))
    slot = _cumsum_lanes(sel) - sel

    s = a.shape[1]
    lane = lax.broadcasted_iota(jnp.int32, a.shape, 1)
    from_right = lambda x, k: pltpu.roll(x, s - k, axis=1)
    tok, wv = lane, a
    owed = jnp.where(sel > 0, lane - slot, NO_TOKEN)
    k = 1
    while k < s:
        in_owed = from_right(owed, k)
        incoming = (in_owed & k) != 0
        tok = jnp.where(incoming, from_right(tok, k), tok)
        wv = jnp.where(incoming, from_right(wv, k), wv)
        owed = jnp.where(incoming, in_owed ^ k, jnp.where((owed & k) != 0, NO_TOKEN, owed))
        k *= 2
    idx_ref[...] = tok[:, :cap]
    w_ref[...] = wv[:, :cap]


def _topk(afft, cap, rows_per_step):
    n, s = afft.shape
    return pl.pallas_call(
        functools.partial(_topk_kernel, cap=cap),
        grid=(n // rows_per_step,),
        in_specs=[pl.BlockSpec((rows_per_step, s), lambda i: (i, 0))],
        out_specs=[pl.BlockSpec((rows_per_step, cap), lambda i: (i, 0)),
                   pl.BlockSpec((rows_per_step, cap), lambda i: (i, 0))],
        out_shape=[jax.ShapeDtypeStruct((n, cap), jnp.int32),
                   jax.ShapeDtypeStruct((n, cap), F32)],
        compiler_params=pltpu.CompilerParams(dimension_semantics=("parallel",)),
        name="topk",
    )(afft)


SCATTER_GROUP = 8


def _experts_kernel(idx_ref, w_ref, h2_ref, wg_ref, wu_ref, wd_ref, out_ref,
                    xg_a, xg_b, ye_a, ye_b, *, cap, n_exp, n_real):
    g = pl.program_id(0)

    def gather(pair, dst):
        base = pair * cap
        for c in range(cap):
            dst[c:c + 1, :] = h2_ref[pl.ds(idx_ref[base + c], 1), :]

    @pl.when(g == 0)
    def _():
        ye_b[...] = jnp.zeros_like(ye_b)
        gather(0, xg_a)

    @pl.when((g == 0) | ((g - 1) % n_exp == 0))
    def _():
        out_ref[...] = jnp.zeros_like(out_ref)

    def step(xg_cur, xg_nxt, ye_cur, ye_prv):
        gather(jnp.minimum(g + 1, n_real - 1), xg_nxt)
        xg = xg_cur[...].astype(BF16)
        hg = jnp.dot(xg, wg_ref[...], preferred_element_type=F32)
        hu = jnp.dot(xg, wu_ref[...], preferred_element_type=F32)
        hid = (hg * _sigmoid(hg) * hu).astype(BF16)
        ye_cur[...] = jnp.dot(hid, wd_ref[...], preferred_element_type=F32)
        base = jnp.maximum(g - 1, 0) * cap
        for c0 in range(0, cap, SCATTER_GROUP):
            toks = [idx_ref[base + c0 + i] for i in range(SCATTER_GROUP)]
            rows = [out_ref[pl.ds(toks[i], 1), :] + ye_prv[c0 + i:c0 + i + 1, :] * w_ref[base + c0 + i]
                    for i in range(SCATTER_GROUP)]
            for i in range(SCATTER_GROUP):
                out_ref[pl.ds(toks[i], 1), :] = rows[i]

    @pl.when(g % 2 == 0)
    def _():
        step(xg_a, xg_b, ye_a, ye_b)

    @pl.when(g % 2 == 1)
    def _():
        step(xg_b, xg_a, ye_b, ye_a)


def _experts(h2, idx, w, wg, wu, wd, cap):
    b, s, d = h2.shape
    ne, _, f = wg.shape
    n_real = b * ne
    last = n_real - 1
    seq_of = lambda pair: pair // ne
    weights = lambda g, idx_, w_: (jnp.minimum(g, last) % ne, 0, 0)
    return pl.pallas_call(
        functools.partial(_experts_kernel, cap=cap, n_exp=ne, n_real=n_real),
        grid_spec=pltpu.PrefetchScalarGridSpec(
            num_scalar_prefetch=2,
            grid=(n_real + 1,),
            in_specs=[
                pl.BlockSpec((None, s, d), lambda g, idx_, w_: (seq_of(jnp.minimum(g + 1, last)), 0, 0)),
                pl.BlockSpec((None, d, f), weights),
                pl.BlockSpec((None, d, f), weights),
                pl.BlockSpec((None, f, d), weights),
            ],
            out_specs=pl.BlockSpec((None, s, d), lambda g, idx_, w_: (seq_of(jnp.maximum(g - 1, 0)), 0, 0)),
            scratch_shapes=[pltpu.VMEM((cap, d), F32)] * 4,
        ),
        out_shape=jax.ShapeDtypeStruct((b, s, d), F32),
        compiler_params=pltpu.CompilerParams(
            dimension_semantics=("arbitrary",), vmem_limit_bytes=EXPERTS_VMEM_LIMIT),
        name="experts",
    )(idx.reshape(-1), w.reshape(-1), h2, wg, wu, wd)


def _final_kernel(x1_ref, moe_ref, p_ref, n3g_ref, wpg_ref, bpg_ref, wple_ref, pg_ref, fg_ref, o_ref,
                  *, last_layer):
    x2 = x1_ref[...] + moe_ref[...]
    hn = _rms(x2, n3g_ref[...]).astype(BF16)
    gate = _sigmoid(jnp.dot(hn, wpg_ref[...], preferred_element_type=F32) + bpg_ref[...])
    emb = jnp.dot(p_ref[...].astype(BF16), wple_ref[...], preferred_element_type=F32)
    x3 = x2 + gate * _rms(emb, pg_ref[...])
    o_ref[...] = _rms(x3, fg_ref[...]) if last_layer else x3


def _final(x1, moe, p2d, n3g, wpg, bpg, wple, pg, fg, tm, last_layer):
    t, d = x1.shape
    q = p2d.shape[1]

    def full(arr):
        nd = arr.ndim
        return pl.BlockSpec(arr.shape, lambda i: (0,) * nd)

    params = (n3g, wpg, bpg, wple, pg, fg)
    return pl.pallas_call(
        functools.partial(_final_kernel, last_layer=last_layer),
        grid=(t // tm,),
        in_specs=[
            pl.BlockSpec((tm, d), lambda i: (i, 0)),
            pl.BlockSpec((tm, d), lambda i: (i, 0)),
            pl.BlockSpec((tm, q), lambda i: (i, 0)),
        ] + [full(a) for a in params],
        out_specs=pl.BlockSpec((tm, d), lambda i: (i, 0)),
        out_shape=jax.ShapeDtypeStruct((t, d), F32),
        compiler_params=pltpu.CompilerParams(
            dimension_semantics=("parallel",), vmem_limit_bytes=VMEM_LIMIT),
        name="final",
    )(x1, moe, p2d, *params)


def _router_pieces(w_router):
    w_hi = w_router.astype(BF16)
    w_lo = (w_router - w_hi.astype(F32)).astype(BF16)
    return jnp.concatenate([w_hi, w_hi, w_lo], axis=0).T


def kernel(x, p, norm1_g, w_in, b_gate, conv_w, conv_b, conv_ln_g, conv_ln_b, w_conv_out, w_pool,
           pool_scale, w_out, norm2_g, w_router, w_exp_gate, w_exp_up, w_exp_down, norm3_g,
           w_ple_gate, b_ple_gate, w_ple, ple_norm_g, final_g):
    b, s, d = x.shape
    depth = w_in.shape[0]
    cap = max(1, CAPACITY_FACTOR * s // N_EXPERTS)
    row = lambda a: a.reshape(1, -1)

    for l in range(depth):
        x1, h2, afft = _mixer(
            x, row(norm1_g[l]), w_in[l].astype(BF16), row(b_gate[l]), conv_w[l], row(conv_b[l]),
            row(conv_ln_g[l]), row(conv_ln_b[l]), w_conv_out[l].astype(BF16),
            w_pool[l].astype(BF16), row(pool_scale[l]), w_out[l].astype(BF16),
            row(norm2_g[l]), _router_pieces(w_router[l]), ts=512, rc=64, n_chains=2)
        idx, wts = _topk(afft.reshape(b * N_EXPERTS, s), cap, rows_per_step=4 * N_EXPERTS)
        moe = _experts(h2, idx, wts, w_exp_gate[l].astype(BF16), w_exp_up[l].astype(BF16),
                       w_exp_down[l].astype(BF16), cap)
        x = _final(x1.reshape(b * s, d), moe.reshape(b * s, d), p[l].reshape(b * s, -1),
                   row(norm3_g[l]), w_ple_gate[l].astype(BF16), row(b_ple_gate[l]),
                   w_ple[l].astype(BF16), row(ple_norm_g[l]),
                   row(final_g), tm=512, last_layer=l == depth - 1).reshape(b, s, d)
    return x
```

```python
import functools

import jax
import jax.numpy as jnp
from jax import lax
from jax.experimental import pallas as pl
from jax.experimental.pallas import tpu as pltpu

F32 = jnp.float32
BF16 = jnp.bfloat16

CONV_DIM = 512
CONV_WIDTH = 31
CONV_HALF = CONV_WIDTH // 2
POOL_DIM = 512
POOL_WINDOWS = (2, 4, 8, 16)
POOL_GROUP_DIM = POOL_DIM // len(POOL_WINDOWS)
N_EXPERTS = 16
CAPACITY_FACTOR = 2
EPS = 1e-6

SUBLANES = 8
HALO = 16
VMEM_LIMIT = 56 * 1024 * 1024
EXPERTS_VMEM_LIMIT = 60 * 1024 * 1024


def _rms(x, g):
    return x * lax.rsqrt(jnp.mean(x * x, axis=-1, keepdims=True) + EPS) * g


def _sigmoid(x):
    return 1.0 / (1.0 + jnp.exp(-x))


def _mixer_kernel(x_ref, xp_ref, xn_ref,
                  n1g_ref, win_ref, bg_ref, cw_ref, cb_ref, lg_ref, lb_ref, wco_ref,
                  wp_ref, ps_ref, wo_ref, n2g_ref, wrt_ref,
                  x1_ref, h2_ref, afft_ref,
                  hext_ref, *chain_scratch, ts, seq, rc, n_chains):
    i = pl.program_id(1)
    nt = pl.num_programs(1)
    d = x_ref.shape[-1]
    c2 = 2 * CONV_DIM
    c3 = c2 + POOL_DIM
    tc = ts // n_chains
    ext = tc + 2 * HALO

    hext_ref[0:HALO, :] = _rms(xp_ref[...], n1g_ref[...]).astype(BF16)
    hext_ref[HALO:HALO + ts, :] = _rms(x_ref[...], n1g_ref[...]).astype(BF16)
    hext_ref[HALO + ts:2 * HALO + ts, :] = _rms(xn_ref[...], n1g_ref[...]).astype(BF16)

    def scratch(j):
        per_chain = len(chain_scratch) // n_chains
        return chain_scratch[j * per_chain:(j + 1) * per_chain]

    def in_proj(j):
        vext_ref, uext_ref, _, _ = scratch(j)
        zc = jnp.dot(hext_ref[j * tc:j * tc + ext, :], win_ref[:, :c3], preferred_element_type=F32)
        vext_ref[...] = zc[:, :CONV_DIM] * _sigmoid(zc[:, CONV_DIM:c2])
        uext_ref[...] = zc[:, c2:]
        if j == 0:
            for ref in (vext_ref, uext_ref):
                ref[0:HALO, :] = jnp.where(i == 0, 0.0, ref[0:HALO, :])
        if j == n_chains - 1:
            for ref in (vext_ref, uext_ref):
                ref[HALO + tc:ext, :] = jnp.where(i == nt - 1, 0.0, ref[HALO + tc:ext, :])

    def gate_logits(j):
        h = hext_ref[HALO + j * tc:HALO + (j + 1) * tc, :]
        return jnp.dot(h, win_ref[:, c3:], preferred_element_type=F32) + bg_ref[...]

    def conv_branch(j):
        vext_ref, _, vsh_ref, conv_ref = scratch(j)
        nsh = vsh_ref.shape[1]
        for ph in range(SUBLANES):
            vsh_ref[ph] = vext_ref[ph:ph + nsh, :]
        for c in range(tc // rc):
            acc = jnp.zeros((rc, CONV_DIM), F32)
            for k in range(CONV_WIDTH):
                off = HALO - CONV_HALF + k
                ph = off % SUBLANES
                r0 = c * rc + off - ph
                acc = acc + vsh_ref[ph, r0:r0 + rc, :] * cw_ref[k:k + 1, :]
            conv_ref[c * rc:(c + 1) * rc, :] = acc + cb_ref[...]
        cv = conv_ref[...]
        mu = jnp.mean(cv, axis=-1, keepdims=True)
        xc = cv - mu
        var = jnp.mean(xc * xc, axis=-1, keepdims=True)
        ln = xc * lax.rsqrt(var + EPS) * lg_ref[...] + lb_ref[...]
        return (ln * _sigmoid(ln)).astype(BF16)

    def merge(j, act, gate_pre):
        _, uext_ref, _, _ = scratch(j)
        rows = slice(j * tc, (j + 1) * tc)
        y_conv = jnp.dot(act, wco_ref[...], preferred_element_type=F32)

        pos = i * ts + j * tc + lax.broadcasted_iota(jnp.int32, (tc, 1), 0)
        pooled = []
        for g, w in enumerate(POOL_WINDOWS):
            lanes = slice(g * POOL_GROUP_DIM, (g + 1) * POOL_GROUP_DIM)
            ue = uext_ref[:, lanes]
            run, width = ue, 1
            while width < w:
                run = run + pltpu.roll(run, width, axis=0)
                width *= 2
            ahead = (w - w // 2) - 1
            if ahead:
                run = pltpu.roll(run, ext - ahead, axis=0)
            lo = jnp.maximum(pos - w // 2, 0)
            hi = jnp.minimum(pos + ahead, seq - 1)
            cnt = (hi - lo + 1).astype(F32)
            dgrp = run[HALO:HALO + tc] / cnt - ue[HALO:HALO + tc]
            pooled.append(jnp.dot(dgrp.astype(BF16), wp_ref[g], preferred_element_type=F32))
        y_pool = jnp.concatenate(pooled, axis=-1) * ps_ref[...]

        gates = _sigmoid(gate_pre)
        merged = gates[:, :d] * y_conv + gates[:, d:] * y_pool
        x1 = x_ref[rows, :] + jnp.dot(merged.astype(BF16), wo_ref[...], preferred_element_type=F32)
        x1_ref[rows, :] = x1

        h2 = _rms(x1, n2g_ref[...])
        h2_ref[rows, :] = h2
        h2_hi = h2.astype(BF16)
        h2_lo = (h2 - h2_hi.astype(F32)).astype(BF16)
        logits_t = lax.dot_general(wrt_ref[...], jnp.concatenate([h2_hi, h2_lo, h2_hi], axis=-1),
                                   (((1,), (1,)), ((), ())), preferred_element_type=F32)
        ex = jnp.exp(logits_t - jnp.max(logits_t, axis=0, keepdims=True))
        afft_ref[:, rows] = ex / jnp.sum(ex, axis=0, keepdims=True)

    in_proj(0)
    pending = None
    for j in range(n_chains):
        if j + 1 < n_chains:
            in_proj(j + 1)
        gate_pre = gate_logits(j)
        act = conv_branch(j)
        if pending is not None:
            merge(*pending)
        pending = (j, act, gate_pre)
    merge(*pending)


def _mixer(x, n1g, win, bg, cw, cb, lg, lb, wco, wp, ps, wo, n2g, wrt, ts, rc, n_chains):
    b, s, d = x.shape
    nt = s // ts
    hb = ts // HALO
    nhb = s // HALO
    tc = ts // n_chains

    def cur(bi, i):
        return (bi, i, 0)

    def prev(bi, i):
        return (bi, jnp.maximum(i * hb - 1, 0), 0)

    def nxt(bi, i):
        return (bi, jnp.minimum((i + 1) * hb, nhb - 1), 0)

    def full(arr):
        nd = arr.ndim
        return pl.BlockSpec(arr.shape, lambda bi, i: (0,) * nd, pipeline_mode=pl.Buffered(1))

    params = (n1g, win, bg, cw, cb, lg, lb, wco, wp, ps, wo, n2g, wrt)
    chain_scratch = [
        pltpu.VMEM((tc + 2 * HALO, CONV_DIM), F32),
        pltpu.VMEM((tc + 2 * HALO, POOL_DIM), F32),
        pltpu.VMEM((SUBLANES, tc + 2 * HALO - SUBLANES, CONV_DIM), F32),
        pltpu.VMEM((tc, CONV_DIM), F32),
    ]
    return pl.pallas_call(
        functools.partial(_mixer_kernel, ts=ts, seq=s, rc=rc, n_chains=n_chains),
        grid=(b, nt),
        in_specs=[
            pl.BlockSpec((None, ts, d), cur),
            pl.BlockSpec((None, HALO, d), prev),
            pl.BlockSpec((None, HALO, d), nxt),
        ] + [full(a) for a in params],
        out_specs=[
            pl.BlockSpec((None, ts, d), cur),
            pl.BlockSpec((None, ts, d), cur),
            pl.BlockSpec((None, N_EXPERTS, ts), lambda bi, i: (bi, 0, i)),
        ],
        out_shape=[
            jax.ShapeDtypeStruct((b, s, d), F32),
            jax.ShapeDtypeStruct((b, s, d), F32),
            jax.ShapeDtypeStruct((b, N_EXPERTS, s), F32),
        ],
        scratch_shapes=[pltpu.VMEM((ts + 2 * HALO, d), BF16)] + chain_scratch * n_chains,
        compiler_params=pltpu.CompilerParams(
            dimension_semantics=("parallel", "parallel"), vmem_limit_bytes=VMEM_LIMIT),
        name="mixer",
    )(x, x, x, *params)


def _cumsum_lanes(x):
    n = x.shape[-1]
    lane = lax.broadcasted_iota(jnp.int32, x.shape, x.ndim - 1)
    step = 1
    while step < n:
        x = x + jnp.where(lane >= step, pltpu.roll(x, step, axis=x.ndim - 1), 0)
        step *= 2
    return x


NO_TOKEN = 1 << 20


def _topk_kernel(a_ref, idx_ref, w_ref, *, cap):
    a = a_ref[...]
    as_float = lambda word: pltpu.bitcast(word, F32)
    thr = jnp.zeros((a.shape[0], 1), jnp.int32)
    for bit in range(30, -1, -1):
        cand = thr | (1 << bit)
        cnt = jnp.sum((a >= as_float(cand)).astype(jnp.int32), axis=1, keepdims=True)
        thr = jnp.where(cnt >= cap, cand, thr)
    gt = a >= as_float(thr + 1)
    eq = jnp.where(gt, 0, (a >= as_float(thr)).astype(jnp.int32))
    need = cap - jnp.sum(gt.astype(jnp.int32), axis=1, keepdims=True)
    eq_rank = _cumsum_lanes(eq) - eq
    sel = jnp.where(gt, 1, jnp.where(eq_rank < need, eq, 0))
    slot = _cumsum_lanes(sel) - sel

    s = a.shape[1]
    lane = lax.broadcasted_iota(jnp.int32, a.shape, 1)
    from_right = lambda x, k: pltpu.roll(x, s - k, axis=1)
    tok, wv = lane, a
    owed = jnp.where(sel > 0, lane - slot, NO_TOKEN)
    k = 1
    while k < s:
        in_owed = from_right(owed, k)
        incoming = (in_owed & k) != 0
        tok = jnp.where(incoming, from_right(tok, k), tok)
        wv = jnp.where(incoming, from_right(wv, k), wv)
        owed = jnp.where(incoming, in_owed ^ k, jnp.where((owed & k) != 0, NO_TOKEN, owed))
        k *= 2
    idx_ref[...] = tok[:, :cap]
    w_ref[...] = wv[:, :cap]


def _topk(afft, cap, rows_per_step):
    n, s = afft.shape
    return pl.pallas_call(
        functools.partial(_topk_kernel, cap=cap),
        grid=(n // rows_per_step,),
        in_specs=[pl.BlockSpec((rows_per_step, s), lambda i: (i, 0))],
        out_specs=[pl.BlockSpec((rows_per_step, cap), lambda i: (i, 0)),
                   pl.BlockSpec((rows_per_step, cap), lambda i: (i, 0))],
        out_shape=[jax.ShapeDtypeStruct((n, cap), jnp.int32),
                   jax.ShapeDtypeStruct((n, cap), F32)],
        compiler_params=pltpu.CompilerParams(dimension_semantics=("parallel",)),
        name="topk",
    )(afft)


SCATTER_GROUP = 8


def _experts_kernel(idx_ref, w_ref,
                    h2_ref, wg_ref, wu_ref, wd_ref, x1_ref, p_ref,
                    n3g_ref, wpg_ref, bpg_ref, wple_ref, pg_ref, fg_ref,
                    o_ref,
                    acc_ref, xg_a, xg_b, ye_a, ye_b, *, cap, n_exp, n_real, lag, last_layer):
    g = pl.program_id(0)
    rows = o_ref.shape[0]

    def gather(pair, dst):
        base = pair * cap
        for c in range(cap):
            dst[c:c + 1, :] = h2_ref[pl.ds(idx_ref[base + c], 1), :]

    @pl.when(g == 0)
    def _():
        acc_ref[...] = jnp.zeros_like(acc_ref)
        ye_b[...] = jnp.zeros_like(ye_b)
        gather(0, xg_a)

    scatter_pair = jnp.maximum(g - 1, 0)
    slot = (scatter_pair // n_exp) % 2

    @pl.when((g - 1) % n_exp == 0)
    def _():
        acc_ref[slot] = jnp.zeros(acc_ref.shape[1:], F32)

    def step(xg_cur, xg_nxt, ye_cur, ye_prv):
        chunk = jnp.clip(g - lag, 0, n_real - 1)
        r0 = pl.multiple_of((chunk % n_exp) * rows, rows)
        x2 = x1_ref[...] + acc_ref[1 - slot, pl.ds(r0, rows), :]
        hn = _rms(x2, n3g_ref[...]).astype(BF16)
        gate = _sigmoid(jnp.dot(hn, wpg_ref[...], preferred_element_type=F32) + bpg_ref[...])
        emb = jnp.dot(p_ref[...].astype(BF16), wple_ref[...], preferred_element_type=F32)
        x3 = x2 + gate * _rms(emb, pg_ref[...])
        o_ref[...] = _rms(x3, fg_ref[...]) if last_layer else x3

        gather(jnp.minimum(g + 1, n_real - 1), xg_nxt)
        xg = xg_cur[...].astype(BF16)
        hg = jnp.dot(xg, wg_ref[...], preferred_element_type=F32)
        hu = jnp.dot(xg, wu_ref[...], preferred_element_type=F32)
        hid = (hg * _sigmoid(hg) * hu).astype(BF16)
        ye_cur[...] = jnp.dot(hid, wd_ref[...], preferred_element_type=F32)

        base = jnp.minimum(scatter_pair, n_real - 1) * cap
        for c0 in range(0, cap, SCATTER_GROUP):
            toks = [idx_ref[base + c0 + i] for i in range(SCATTER_GROUP)]
            new = [acc_ref[slot, pl.ds(toks[i], 1), :] + ye_prv[c0 + i:c0 + i + 1, :] * w_ref[base + c0 + i]
                   for i in range(SCATTER_GROUP)]
            for i in range(SCATTER_GROUP):
                acc_ref[slot, pl.ds(toks[i], 1), :] = new[i]

    @pl.when(g % 2 == 0)
    def _():
        step(xg_a, xg_b, ye_a, ye_b)

    @pl.when(g % 2 == 1)
    def _():
        step(xg_b, xg_a, ye_b, ye_a)


def _experts_final(h2, idx, w, wg, wu, wd, x1, p, n3g, wpg, bpg, wple, pg, fg, cap, last_layer):
    b, s, d = h2.shape
    ne, _, f = wg.shape
    q = p.shape[-1]
    n_real = b * ne
    last = n_real - 1
    lag = ne + 1
    rows = s // ne
    seq_of = lambda pair: pair // ne
    weights = lambda g, idx_, w_: (jnp.minimum(g, last) % ne, 0, 0)

    def chunk(g, idx_, w_):
        c = jnp.clip(g - lag, 0, last)
        return (c // ne, c % ne, 0)

    def full(arr):
        nd = arr.ndim
        return pl.BlockSpec(arr.shape, lambda g, idx_, w_: (0,) * nd, pipeline_mode=pl.Buffered(1))

    params = (n3g, wpg, bpg, wple, pg, fg)
    return pl.pallas_call(
        functools.partial(_experts_kernel, cap=cap, n_exp=ne, n_real=n_real, lag=lag,
                          last_layer=last_layer),
        grid_spec=pltpu.PrefetchScalarGridSpec(
            num_scalar_prefetch=2,
            grid=(n_real + lag,),
            in_specs=[
                pl.BlockSpec((None, s, d), lambda g, idx_, w_: (seq_of(jnp.minimum(g + 1, last)), 0, 0)),
                pl.BlockSpec((None, d, f), weights),
                pl.BlockSpec((None, d, f), weights),
                pl.BlockSpec((None, f, d), weights),
                pl.BlockSpec((None, rows, d), chunk),
                pl.BlockSpec((None, rows, q), chunk),
            ] + [full(a) for a in params],
            out_specs=pl.BlockSpec((None, rows, d), chunk),
            scratch_shapes=[pltpu.VMEM((2, s, d), F32)] + [pltpu.VMEM((cap, d), F32)] * 4,
        ),
        out_shape=jax.ShapeDtypeStruct((b, s, d), F32),
        compiler_params=pltpu.CompilerParams(
            dimension_semantics=("arbitrary",), vmem_limit_bytes=EXPERTS_VMEM_LIMIT),
        name="experts",
    )(idx.reshape(-1), w.reshape(-1), h2, wg, wu, wd, x1, p, *params)


def _router_pieces(w_router):
    w_hi = w_router.astype(BF16)
    w_lo = (w_router - w_hi.astype(F32)).astype(BF16)
    return jnp.concatenate([w_hi, w_hi, w_lo], axis=0).T


def kernel(x, p, norm1_g, w_in, b_gate, conv_w, conv_b, conv_ln_g, conv_ln_b, w_conv_out, w_pool,
           pool_scale, w_out, norm2_g, w_router, w_exp_gate, w_exp_up, w_exp_down, norm3_g,
           w_ple_gate, b_ple_gate, w_ple, ple_norm_g, final_g):
    b, s, d = x.shape
    depth = w_in.shape[0]
    cap = max(1, CAPACITY_FACTOR * s // N_EXPERTS)
    row = lambda a: a.reshape(1, -1)

    for l in range(depth):
        x1, h2, afft = _mixer(
            x, row(norm1_g[l]), w_in[l].astype(BF16), row(b_gate[l]), conv_w[l], row(conv_b[l]),
            row(conv_ln_g[l]), row(conv_ln_b[l]), w_conv_out[l].astype(BF16),
            w_pool[l].astype(BF16), row(pool_scale[l]), w_out[l].astype(BF16),
            row(norm2_g[l]), _router_pieces(w_router[l]), ts=1024, rc=64, n_chains=4)
        idx, wts = _topk(afft.reshape(b * N_EXPERTS, s), cap, rows_per_step=4 * N_EXPERTS)
        x = _experts_final(
            h2, idx, wts, w_exp_gate[l].astype(BF16), w_exp_up[l].astype(BF16),
            w_exp_down[l].astype(BF16), x1, p[l], row(norm3_g[l]), w_ple_gate[l].astype(BF16),
            row(b_ple_gate[l]), w_ple[l].astype(BF16), row(ple_norm_g[l]), row(final_g), cap,
            last_layer=l == depth - 1)
    return x
```

```python
import functools

import jax
import jax.numpy as jnp
from jax import lax
from jax.experimental import pallas as pl
from jax.experimental.pallas import tpu as pltpu

F32 = jnp.float32
BF16 = jnp.bfloat16

CONV_DIM = 512
CONV_WIDTH = 31
CONV_HALF = CONV_WIDTH // 2
POOL_DIM = 512
POOL_WINDOWS = (2, 4, 8, 16)
POOL_GROUP_DIM = POOL_DIM // len(POOL_WINDOWS)
N_EXPERTS = 16
CAPACITY_FACTOR = 2
EPS = 1e-6

SUBLANES = 8
HALO = 16
VMEM_LIMIT = 56 * 1024 * 1024
EXPERTS_VMEM_LIMIT = 60 * 1024 * 1024


def _rms(x, g):
    return x * lax.rsqrt(jnp.mean(x * x, axis=-1, keepdims=True) + EPS) * g


def _sigmoid(x):
    return 1.0 / (1.0 + jnp.exp(-x))


def _mixer_kernel(x_ref, xp_ref, xn_ref,
                  n1g_ref, win_ref, bg_ref, cw_ref, cb_ref, lg_ref, lb_ref, wco_ref,
                  wp_ref, ps_ref, wo_ref, n2g_ref, wrt_ref,
                  x1_ref, h2_ref, afft_ref,
                  hext_ref, *chain_scratch, ts, seq, rc, n_chains):
    i = pl.program_id(1)
    nt = pl.num_programs(1)
    d = x_ref.shape[-1]
    c2 = 2 * CONV_DIM
    c3 = c2 + POOL_DIM
    tc = ts // n_chains
    ext = tc + 2 * HALO

    hext_ref[0:HALO, :] = _rms(xp_ref[...], n1g_ref[...]).astype(BF16)
    hext_ref[HALO:HALO + ts, :] = _rms(x_ref[...], n1g_ref[...]).astype(BF16)
    hext_ref[HALO + ts:2 * HALO + ts, :] = _rms(xn_ref[...], n1g_ref[...]).astype(BF16)

    def scratch(j):
        per_chain = len(chain_scratch) // n_chains
        return chain_scratch[j * per_chain:(j + 1) * per_chain]

    def in_proj(j):
        vext_ref, uext_ref, _, _ = scratch(j)
        zc = jnp.dot(hext_ref[j * tc:j * tc + ext, :], win_ref[:, :c3], preferred_element_type=F32)
        vext_ref[...] = zc[:, :CONV_DIM] * _sigmoid(zc[:, CONV_DIM:c2])
        uext_ref[...] = zc[:, c2:]
        if j == 0:
            for ref in (vext_ref, uext_ref):
                ref[0:HALO, :] = jnp.where(i == 0, 0.0, ref[0:HALO, :])
        if j == n_chains - 1:
            for ref in (vext_ref, uext_ref):
                ref[HALO + tc:ext, :] = jnp.where(i == nt - 1, 0.0, ref[HALO + tc:ext, :])

    def gate_logits(j):
        h = hext_ref[HALO + j * tc:HALO + (j + 1) * tc, :]
        return jnp.dot(h, win_ref[:, c3:], preferred_element_type=F32) + bg_ref[...]

    def conv_branch(j):
        vext_ref, _, vsh_ref, conv_ref = scratch(j)
        nsh = vsh_ref.shape[1]
        for ph in range(SUBLANES):
            vsh_ref[ph] = vext_ref[ph:ph + nsh, :]
        for c in range(tc // rc):
            acc = jnp.zeros((rc, CONV_DIM), F32)
            for k in range(CONV_WIDTH):
                off = HALO - CONV_HALF + k
                ph = off % SUBLANES
                r0 = c * rc + off - ph
                acc = acc + vsh_ref[ph, r0:r0 + rc, :] * cw_ref[k:k + 1, :]
            conv_ref[c * rc:(c + 1) * rc, :] = acc + cb_ref[...]
        cv = conv_ref[...]
        mu = jnp.mean(cv, axis=-1, keepdims=True)
        xc = cv - mu
        var = jnp.mean(xc * xc, axis=-1, keepdims=True)
        ln = xc * lax.rsqrt(var + EPS) * lg_ref[...] + lb_ref[...]
        return (ln * _sigmoid(ln)).astype(BF16)

    def merge(j, act, gate_pre):
        _, uext_ref, _, _ = scratch(j)
        rows = slice(j * tc, (j + 1) * tc)
        y_conv = jnp.dot(act, wco_ref[...], preferred_element_type=F32)

        pos = i * ts + j * tc + lax.broadcasted_iota(jnp.int32, (tc, 1), 0)
        pooled = []
        for g, w in enumerate(POOL_WINDOWS):
            lanes = slice(g * POOL_GROUP_DIM, (g + 1) * POOL_GROUP_DIM)
            ue = uext_ref[:, lanes]
            run, width = ue, 1
            while width < w:
                run = run + pltpu.roll(run, width, axis=0)
                width *= 2
            ahead = (w - w // 2) - 1
            if ahead:
                run = pltpu.roll(run, ext - ahead, axis=0)
            lo = jnp.maximum(pos - w // 2, 0)
            hi = jnp.minimum(pos + ahead, seq - 1)
            cnt = (hi - lo + 1).astype(F32)
            dgrp = run[HALO:HALO + tc] / cnt - ue[HALO:HALO + tc]
            pooled.append(jnp.dot(dgrp.astype(BF16), wp_ref[g], preferred_element_type=F32))
        y_pool = jnp.concatenate(pooled, axis=-1) * ps_ref[...]

        gates = _sigmoid(gate_pre)
        merged = gates[:, :d] * y_conv + gates[:, d:] * y_pool
        x1 = x_ref[rows, :] + jnp.dot(merged.astype(BF16), wo_ref[...], preferred_element_type=F32)
        x1_ref[rows, :] = x1

        h2 = _rms(x1, n2g_ref[...])
        h2_ref[rows, :] = h2
        h2_hi = h2.astype(BF16)
        h2_lo = (h2 - h2_hi.astype(F32)).astype(BF16)
        logits_t = lax.dot_general(wrt_ref[...], jnp.concatenate([h2_hi, h2_lo, h2_hi], axis=-1),
                                   (((1,), (1,)), ((), ())), preferred_element_type=F32)
        ex = jnp.exp(logits_t - jnp.max(logits_t, axis=0, keepdims=True))
        afft_ref[:, rows] = ex / jnp.sum(ex, axis=0, keepdims=True)

    in_proj(0)
    pending = None
    for j in range(n_chains):
        if j + 1 < n_chains:
            in_proj(j + 1)
        gate_pre = gate_logits(j)
        act = conv_branch(j)
        if pending is not None:
            merge(*pending)
        pending = (j, act, gate_pre)
    merge(*pending)


def _mixer(x, n1g, win, bg, cw, cb, lg, lb, wco, wp, ps, wo, n2g, wrt, ts, rc, n_chains):
    b, s, d = x.shape
    nt = s // ts
    hb = ts // HALO
    nhb = s // HALO
    tc = ts // n_chains

    def cur(bi, i):
        return (bi, i, 0)

    def prev(bi, i):
        return (bi, jnp.maximum(i * hb - 1, 0), 0)

    def nxt(bi, i):
        return (bi, jnp.minimum((i + 1) * hb, nhb - 1), 0)

    def full(arr):
        nd = arr.ndim
        return pl.BlockSpec(arr.shape, lambda bi, i: (0,) * nd, pipeline_mode=pl.Buffered(1))

    params = (n1g, win, bg, cw, cb, lg, lb, wco, wp, ps, wo, n2g, wrt)
    chain_scratch = [
        pltpu.VMEM((tc + 2 * HALO, CONV_DIM), F32),
        pltpu.VMEM((tc + 2 * HALO, POOL_DIM), F32),
        pltpu.VMEM((SUBLANES, tc + 2 * HALO - SUBLANES, CONV_DIM), F32),
        pltpu.VMEM((tc, CONV_DIM), F32),
    ]
    return pl.pallas_call(
        functools.partial(_mixer_kernel, ts=ts, seq=s, rc=rc, n_chains=n_chains),
        grid=(b, nt),
        in_specs=[
            pl.BlockSpec((None, ts, d), cur),
            pl.BlockSpec((None, HALO, d), prev),
            pl.BlockSpec((None, HALO, d), nxt),
        ] + [full(a) for a in params],
        out_specs=[
            pl.BlockSpec((None, ts, d), cur),
            pl.BlockSpec((None, ts, d), cur),
            pl.BlockSpec((None, N_EXPERTS, ts), lambda bi, i: (bi, 0, i)),
        ],
        out_shape=[
            jax.ShapeDtypeStruct((b, s, d), F32),
            jax.ShapeDtypeStruct((b, s, d), F32),
            jax.ShapeDtypeStruct((b, N_EXPERTS, s), F32),
        ],
        scratch_shapes=[pltpu.VMEM((ts + 2 * HALO, d), BF16)] + chain_scratch * n_chains,
        compiler_params=pltpu.CompilerParams(
            dimension_semantics=("parallel", "parallel"), vmem_limit_bytes=VMEM_LIMIT),
        name="mixer",
    )(x, x, x, *params)


def _cumsum_lanes(x):
    n = x.shape[-1]
    lane = lax.broadcasted_iota(jnp.int32, x.shape, x.ndim - 1)
    step = 1
    while step < n:
        x = x + jnp.where(lane >= step, pltpu.roll(x, step, axis=x.ndim - 1), 0)
        step *= 2
    return x


NO_TOKEN = 1 << 20


def _topk_kernel(a_ref, idx_ref, w_ref, *, cap):
    a = a_ref[...]
    as_float = lambda word: pltpu.bitcast(word, F32)
    thr = jnp.zeros((a.shape[0], 1), jnp.int32)
    for bit in range(30, -1, -1):
        cand = thr | (1 << bit)
        cnt = jnp.sum((a >= as_float(cand)).astype(jnp.int32), axis=1, keepdims=True)
        thr = jnp.where(cnt >= cap, cand, thr)
    gt = a >= as_float(thr + 1)
    eq = jnp.where(gt, 0, (a >= as_float(thr)).astype(jnp.int32))
    need = cap - jnp.sum(gt.astype(jnp.int32), axis=1, keepdims=True)
    eq_rank = _cumsum_lanes(eq) - eq
    sel = jnp.where(gt, 1, jnp.where(eq_rank < need, eq, 0))
    slot = _cumsum_lanes(sel) - sel

    s = a.shape[1]
    lane = lax.broadcasted_iota(jnp.int32, a.shape, 1)
    from_right = lambda x, k: pltpu.roll(x, s - k, axis=1)
    tok, wv = lane, a
    owed = jnp.where(sel > 0, lane - slot, NO_TOKEN)
    k = 1
    while k < s:
        in_owed = from_right(owed, k)
        incoming = (in_owed & k) != 0
        tok = jnp.where(incoming, from_right(tok, k), tok)
        wv = jnp.where(incoming, from_right(wv, k), wv)
        owed = jnp.where(incoming, in_owed ^ k, jnp.where((owed & k) != 0, NO_TOKEN, owed))
        k *= 2
    idx_ref[...] = tok[:, :cap]
    w_ref[...] = wv[:, :cap]


def _topk(afft, cap, rows_per_step):
    n, s = afft.shape
    return pl.pallas_call(
        functools.partial(_topk_kernel, cap=cap),
        grid=(n // rows_per_step,),
        in_specs=[pl.BlockSpec((rows_per_step, s), lambda i: (i, 0))],
        out_specs=[pl.BlockSpec((rows_per_step, cap), lambda i: (i, 0)),
                   pl.BlockSpec((rows_per_step, cap), lambda i: (i, 0))],
        out_shape=[jax.ShapeDtypeStruct((n, cap), jnp.int32),
                   jax.ShapeDtypeStruct((n, cap), F32)],
        compiler_params=pltpu.CompilerParams(dimension_semantics=("parallel",)),
        name="topk",
    )(afft)


WEIGHT_PARTS = 4
SCATTER_GROUP = 8


def _experts_kernel(idx_ref, w_ref, h2_ref, *refs, cap, n_exp, n_real):
    g = pl.program_id(0)
    wg_parts, wu_parts, wd_parts = (refs[i * WEIGHT_PARTS:(i + 1) * WEIGHT_PARTS] for i in range(3))
    out_ref, xg_a, xg_b, ye_a, ye_b = refs[3 * WEIGHT_PARTS:]
    whole = lambda parts: jnp.concatenate([part[...] for part in parts], axis=0)

    def gather(pair, dst):
        base = pair * cap
        for c in range(cap):
            dst[c:c + 1, :] = h2_ref[pl.ds(idx_ref[base + c], 1), :]

    @pl.when(g == 0)
    def _():
        ye_b[...] = jnp.zeros_like(ye_b)
        gather(0, xg_a)

    @pl.when((g == 0) | ((g - 1) % n_exp == 0))
    def _():
        out_ref[...] = jnp.zeros_like(out_ref)

    def step(xg_cur, xg_nxt, ye_cur, ye_prv):
        gather(jnp.minimum(g + 1, n_real - 1), xg_nxt)
        xg = xg_cur[...].astype(BF16)
        hg = jnp.dot(xg, whole(wg_parts), preferred_element_type=F32)
        hu = jnp.dot(xg, whole(wu_parts), preferred_element_type=F32)
        hid = (hg * _sigmoid(hg) * hu).astype(BF16)
        ye_cur[...] = jnp.dot(hid, whole(wd_parts), preferred_element_type=F32)
        base = jnp.maximum(g - 1, 0) * cap
        for c0 in range(0, cap, SCATTER_GROUP):
            toks = [idx_ref[base + c0 + i] for i in range(SCATTER_GROUP)]
            rows = [out_ref[pl.ds(toks[i], 1), :] + ye_prv[c0 + i:c0 + i + 1, :] * w_ref[base + c0 + i]
                    for i in range(SCATTER_GROUP)]
            for i in range(SCATTER_GROUP):
                out_ref[pl.ds(toks[i], 1), :] = rows[i]

    @pl.when(g % 2 == 0)
    def _():
        step(xg_a, xg_b, ye_a, ye_b)

    @pl.when(g % 2 == 1)
    def _():
        step(xg_b, xg_a, ye_b, ye_a)


def _experts(h2, idx, w, wg, wu, wd, cap):
    b, s, d = h2.shape
    ne, _, f = wg.shape
    n_real = b * ne
    last = n_real - 1
    seq_of = lambda pair: pair // ne

    def weight_parts(rows_total, cols):
        return [pl.BlockSpec((None, rows_total // WEIGHT_PARTS, cols),
                             lambda g, idx_, w_, part=part: (jnp.minimum(g, last) % ne, part, 0))
                for part in range(WEIGHT_PARTS)]

    return pl.pallas_call(
        functools.partial(_experts_kernel, cap=cap, n_exp=ne, n_real=n_real),
        grid_spec=pltpu.PrefetchScalarGridSpec(
            num_scalar_prefetch=2,
            grid=(n_real + 1,),
            in_specs=[
                pl.BlockSpec((None, s, d), lambda g, idx_, w_: (seq_of(jnp.minimum(g + 1, last)), 0, 0)),
            ] + weight_parts(d, f) + weight_parts(d, f) + weight_parts(f, d),
            out_specs=pl.BlockSpec((None, s, d), lambda g, idx_, w_: (seq_of(jnp.maximum(g - 1, 0)), 0, 0)),
            scratch_shapes=[pltpu.VMEM((cap, d), F32)] * 4,
        ),
        out_shape=jax.ShapeDtypeStruct((b, s, d), F32),
        compiler_params=pltpu.CompilerParams(
            dimension_semantics=("arbitrary",), vmem_limit_bytes=EXPERTS_VMEM_LIMIT),
        name="experts",
    )(idx.reshape(-1), w.reshape(-1), h2, *([wg] * WEIGHT_PARTS + [wu] * WEIGHT_PARTS + [wd] * WEIGHT_PARTS))


def _final_kernel(x1_ref, moe_ref, p_ref, n3g_ref, wpg_ref, bpg_ref, wple_ref, pg_ref, fg_ref, o_ref,
                  *, last_layer):
    x2 = x1_ref[...] + moe_ref[...]
    hn = _rms(x2, n3g_ref[...]).astype(BF16)
    gate = _sigmoid(jnp.dot(hn, wpg_ref[...], preferred_element_type=F32) + bpg_ref[...])
    emb = jnp.dot(p_ref[...].astype(BF16), wple_ref[...], preferred_element_type=F32)
    x3 = x2 + gate * _rms(emb, pg_ref[...])
    o_ref[...] = _rms(x3, fg_ref[...]) if last_layer else x3


def _final(x1, moe, p2d, n3g, wpg, bpg, wple, pg, fg, tm, last_layer):
    t, d = x1.shape
    q = p2d.shape[1]

    def full(arr):
        nd = arr.ndim
        return pl.BlockSpec(arr.shape, lambda i: (0,) * nd)

    params = (n3g, wpg, bpg, wple, pg, fg)
    return pl.pallas_call(
        functools.partial(_final_kernel, last_layer=last_layer),
        grid=(t // tm,),
        in_specs=[
            pl.BlockSpec((tm, d), lambda i: (i, 0)),
            pl.BlockSpec((tm, d), lambda i: (i, 0)),
            pl.BlockSpec((tm, q), lambda i: (i, 0)),
        ] + [full(a) for a in params],
        out_specs=pl.BlockSpec((tm, d), lambda i: (i, 0)),
        out_shape=jax.ShapeDtypeStruct((t, d), F32),
        compiler_params=pltpu.CompilerParams(
            dimension_semantics=("parallel",), vmem_limit_bytes=VMEM_LIMIT),
        name="final",
    )(x1, moe, p2d, *params)


def _router_pieces(w_router):
    w_hi = w_router.astype(BF16)
    w_lo = (w_router - w_hi.astype(F32)).astype(BF16)
    return jnp.concatenate([w_hi, w_hi, w_lo], axis=0).T


def kernel(x, p, norm1_g, w_in, b_gate, conv_w, conv_b, conv_ln_g, conv_ln_b, w_conv_out, w_pool,
           pool_scale, w_out, norm2_g, w_router, w_exp_gate, w_exp_up, w_exp_down, norm3_g,
           w_ple_gate, b_ple_gate, w_ple, ple_norm_g, final_g):
    b, s, d = x.shape
    depth = w_in.shape[0]
    cap = max(1, CAPACITY_FACTOR * s // N_EXPERTS)
    row = lambda a: a.reshape(1, -1)

    for l in range(depth):
        x1, h2, afft = _mixer(
            x, row(norm1_g[l]), w_in[l].astype(BF16), row(b_gate[l]), conv_w[l], row(conv_b[l]),
            row(conv_ln_g[l]), row(conv_ln_b[l]), w_conv_out[l].astype(BF16),
            w_pool[l].astype(BF16), row(pool_scale[l]), w_out[l].astype(BF16),
            row(norm2_g[l]), _router_pieces(w_router[l]), ts=1024, rc=64, n_chains=4)
        idx, wts = _topk(afft.reshape(b * N_EXPERTS, s), cap, rows_per_step=4 * N_EXPERTS)
        moe = _experts(h2, idx, wts, w_exp_gate[l].astype(BF16), w_exp_up[l].astype(BF16),
                       w_exp_down[l].astype(BF16), cap)
        x = _final(x1.reshape(b * s, d), moe.reshape(b * s, d), p[l].reshape(b * s, -1),
                   row(norm3_g[l]), w_ple_gate[l].astype(BF16), row(b_ple_gate[l]),
                   w_ple[l].astype(BF16), row(ple_norm_g[l]),
                   row(final_g), tm=512, last_layer=l == depth - 1).reshape(b, s, d)
    return x
```

```python
import functools

import jax
import jax.numpy as jnp
from jax import lax
from jax.experimental import pallas as pl
from jax.experimental.pallas import tpu as pltpu

F32 = jnp.float32
BF16 = jnp.bfloat16

CONV_DIM = 512
CONV_WIDTH = 31
CONV_HALF = CONV_WIDTH // 2
POOL_DIM = 512
POOL_WINDOWS = (2, 4, 8, 16)
POOL_GROUP_DIM = POOL_DIM // len(POOL_WINDOWS)
N_EXPERTS = 16
CAPACITY_FACTOR = 2
EPS = 1e-6

SUBLANES = 8
HALO = 16
VMEM_LIMIT = 56 * 1024 * 1024
EXPERTS_VMEM_LIMIT = 60 * 1024 * 1024


def _rms(x, g):
    return x * lax.rsqrt(jnp.mean(x * x, axis=-1, keepdims=True) + EPS) * g


def _sigmoid(x):
    return 1.0 / (1.0 + jnp.exp(-x))


def _mixer_kernel(x_ref, xp_ref, xn_ref,
                  n1g_ref, win_ref, bg_ref, cw_ref, cb_ref, lg_ref, lb_ref, wco_ref,
                  wp_ref, ps_ref, wo_ref, n2g_ref, wrt_ref,
                  x1_ref, h2_ref, afft_ref,
                  hext_ref, *chain_scratch, ts, seq, rc, n_chains):
    i = pl.program_id(1)
    nt = pl.num_programs(1)
    d = x_ref.shape[-1]
    c2 = 2 * CONV_DIM
    c3 = c2 + POOL_DIM
    tc = ts // n_chains
    ext = tc + 2 * HALO

    hext_ref[0:HALO, :] = _rms(xp_ref[...], n1g_ref[...]).astype(BF16)
    hext_ref[HALO:HALO + ts, :] = _rms(x_ref[...], n1g_ref[...]).astype(BF16)
    hext_ref[HALO + ts:2 * HALO + ts, :] = _rms(xn_ref[...], n1g_ref[...]).astype(BF16)

    def scratch(j):
        per_chain = len(chain_scratch) // n_chains
        return chain_scratch[j * per_chain:(j + 1) * per_chain]

    def in_proj(j):
        vext_ref, uext_ref, _, _ = scratch(j)
        zc = jnp.dot(hext_ref[j * tc:j * tc + ext, :], win_ref[:, :c3], preferred_element_type=F32)
        vext_ref[...] = zc[:, :CONV_DIM] * _sigmoid(zc[:, CONV_DIM:c2])
        uext_ref[...] = zc[:, c2:]
        if j == 0:
            for ref in (vext_ref, uext_ref):
                ref[0:HALO, :] = jnp.where(i == 0, 0.0, ref[0:HALO, :])
        if j == n_chains - 1:
            for ref in (vext_ref, uext_ref):
                ref[HALO + tc:ext, :] = jnp.where(i == nt - 1, 0.0, ref[HALO + tc:ext, :])

    def gate_logits(j):
        h = hext_ref[HALO + j * tc:HALO + (j + 1) * tc, :]
        return jnp.dot(h, win_ref[:, c3:], preferred_element_type=F32) + bg_ref[...]

    def conv_branch(j):
        vext_ref, _, vsh_ref, conv_ref = scratch(j)
        nsh = vsh_ref.shape[1]
        for ph in range(SUBLANES):
            vsh_ref[ph] = vext_ref[ph:ph + nsh, :]
        for c in range(tc // rc):
            acc = jnp.zeros((rc, CONV_DIM), F32)
            for k in range(CONV_WIDTH):
                off = HALO - CONV_HALF + k
                ph = off % SUBLANES
                r0 = c * rc + off - ph
                acc = acc + vsh_ref[ph, r0:r0 + rc, :] * cw_ref[k:k + 1, :]
            conv_ref[c * rc:(c + 1) * rc, :] = acc + cb_ref[...]
        cv = conv_ref[...]
        mu = jnp.mean(cv, axis=-1, keepdims=True)
        xc = cv - mu
        var = jnp.mean(xc * xc, axis=-1, keepdims=True)
        ln = xc * lax.rsqrt(var + EPS) * lg_ref[...] + lb_ref[...]
        return (ln * _sigmoid(ln)).astype(BF16)

    def merge(j, act, gate_pre):
        _, uext_ref, _, _ = scratch(j)
        rows = slice(j * tc, (j + 1) * tc)
        y_conv = jnp.dot(act, wco_ref[...], preferred_element_type=F32)

        pos = i * ts + j * tc + lax.broadcasted_iota(jnp.int32, (tc, 1), 0)
        pooled = []
        for g, w in enumerate(POOL_WINDOWS):
            lanes = slice(g * POOL_GROUP_DIM, (g + 1) * POOL_GROUP_DIM)
            ue = uext_ref[:, lanes]
            run, width = ue, 1
            while width < w:
                run = run + pltpu.roll(run, width, axis=0)
                width *= 2
            ahead = (w - w // 2) - 1
            if ahead:
                run = pltpu.roll(run, ext - ahead, axis=0)
            lo = jnp.maximum(pos - w // 2, 0)
            hi = jnp.minimum(pos + ahead, seq - 1)
            cnt = (hi - lo + 1).astype(F32)
            dgrp = run[HALO:HALO + tc] / cnt - ue[HALO:HALO + tc]
            pooled.append(jnp.dot(dgrp.astype(BF16), wp_ref[g], preferred_element_type=F32))
        y_pool = jnp.concatenate(pooled, axis=-1) * ps_ref[...]

        gates = _sigmoid(gate_pre)
        merged = gates[:, :d] * y_conv + gates[:, d:] * y_pool
        x1 = x_ref[rows, :] + jnp.dot(merged.astype(BF16), wo_ref[...], preferred_element_type=F32)
        x1_ref[rows, :] = x1

        h2 = _rms(x1, n2g_ref[...])
        h2_ref[rows, :] = h2
        h2_hi = h2.astype(BF16)
        h2_lo = (h2 - h2_hi.astype(F32)).astype(BF16)
        logits_t = lax.dot_general(wrt_ref[...], jnp.concatenate([h2_hi, h2_lo, h2_hi], axis=-1),
                                   (((1,), (1,)), ((), ())), preferred_element_type=F32)
        ex = jnp.exp(logits_t - jnp.max(logits_t, axis=0, keepdims=True))
        afft_ref[:, rows] = ex / jnp.sum(ex, axis=0, keepdims=True)

    in_proj(0)
    pending = None
    for j in range(n_chains):
        if j + 1 < n_chains:
            in_proj(j + 1)
        gate_pre = gate_logits(j)
        act = conv_branch(j)
        if pending is not None:
            merge(*pending)
        pending = (j, act, gate_pre)
    merge(*pending)


def _mixer(x, n1g, win, bg, cw, cb, lg, lb, wco, wp, ps, wo, n2g, wrt, ts, rc, n_chains):
    b, s, d = x.shape
    nt = s // ts
    hb = ts // HALO
    nhb = s // HALO
    tc = ts // n_chains

    def cur(bi, i):
        return (bi, i, 0)

    def prev(bi, i):
        return (bi, jnp.maximum(i * hb - 1, 0), 0)

    def nxt(bi, i):
        return (bi, jnp.minimum((i + 1) * hb, nhb - 1), 0)

    def full(arr):
        nd = arr.ndim
        return pl.BlockSpec(arr.shape, lambda bi, i: (0,) * nd, pipeline_mode=pl.Buffered(1))

    params = (n1g, win, bg, cw, cb, lg, lb, wco, wp, ps, wo, n2g, wrt)
    chain_scratch = [
        pltpu.VMEM((tc + 2 * HALO, CONV_DIM), F32),
        pltpu.VMEM((tc + 2 * HALO, POOL_DIM), F32),
        pltpu.VMEM((SUBLANES, tc + 2 * HALO - SUBLANES, CONV_DIM), F32),
        pltpu.VMEM((tc, CONV_DIM), F32),
    ]
    return pl.pallas_call(
        functools.partial(_mixer_kernel, ts=ts, seq=s, rc=rc, n_chains=n_chains),
        grid=(b, nt),
        in_specs=[
            pl.BlockSpec((None, ts, d), cur),
            pl.BlockSpec((None, HALO, d), prev),
            pl.BlockSpec((None, HALO, d), nxt),
        ] + [full(a) for a in params],
        out_specs=[
            pl.BlockSpec((None, ts, d), cur),
            pl.BlockSpec((None, ts, d), cur),
            pl.BlockSpec((None, N_EXPERTS, ts), lambda bi, i: (bi, 0, i)),
        ],
        out_shape=[
            jax.ShapeDtypeStruct((b, s, d), F32),
            jax.ShapeDtypeStruct((b, s, d), F32),
            jax.ShapeDtypeStruct((b, N_EXPERTS, s), F32),
        ],
        scratch_shapes=[pltpu.VMEM((ts + 2 * HALO, d), BF16)] + chain_scratch * n_chains,
        compiler_params=pltpu.CompilerParams(
            dimension_semantics=("parallel", "parallel"), vmem_limit_bytes=VMEM_LIMIT),
        name="mixer",
    )(x, x, x, *params)


def _cumsum_lanes(x):
    n = x.shape[-1]
    lane = lax.broadcasted_iota(jnp.int32, x.shape, x.ndim - 1)
    step = 1
    while step < n:
        x = x + jnp.where(lane >= step, pltpu.roll(x, step, axis=x.ndim - 1), 0)
        step *= 2
    return x


NO_TOKEN = 1 << 20


def _topk_kernel(a_ref, idx_ref, w_ref, *, cap):
    a = a_ref[...]
    as_float = lambda word: pltpu.bitcast(word, F32)
    thr = jnp.zeros((a.shape[0], 1), jnp.int32)
    for bit in range(30, -1, -1):
        cand = thr | (1 << bit)
        cnt = jnp.sum((a >= as_float(cand)).astype(jnp.int32), axis=1, keepdims=True)
        thr = jnp.where(cnt >= cap, cand, thr)
    gt = a >= as_float(thr + 1)
    eq = jnp.where(gt, 0, (a >= as_float(thr)).astype(jnp.int32))
    need = cap - jnp.sum(gt.astype(jnp.int32), axis=1, keepdims=True)
    eq_rank = _cumsum_lanes(eq) - eq
    sel = jnp.where(gt, 1, jnp.where(eq_rank < need, eq, 0))
    slot = _cumsum_lanes(sel) - sel

    s = a.shape[1]
    lane = lax.broadcasted_iota(jnp.int32, a.shape, 1)
    from_right = lambda x, k: pltpu.roll(x, s - k, axis=1)
    tok, wv = lane, a
    owed = jnp.where(sel > 0, lane - slot, NO_TOKEN)
    k = 1
    while k < s:
        in_owed = from_right(owed, k)
        incoming = (in_owed & k) != 0
        tok = jnp.where(incoming, from_right(tok, k), tok)
        wv = jnp.where(incoming, from_right(wv, k), wv)
        owed = jnp.where(incoming, in_owed ^ k, jnp.where((owed & k) != 0, NO_TOKEN, owed))
        k *= 2
    idx_ref[...] = tok[:, :cap]
    w_ref[...] = wv[:, :cap]


def _topk(afft, cap, rows_per_step):
    n, s = afft.shape
    return pl.pallas_call(
        functools.partial(_topk_kernel, cap=cap),
        grid=(n // rows_per_step,),
        in_specs=[pl.BlockSpec((rows_per_step, s), lambda i: (i, 0))],
        out_specs=[pl.BlockSpec((rows_per_step, cap), lambda i: (i, 0)),
                   pl.BlockSpec((rows_per_step, cap), lambda i: (i, 0))],
        out_shape=[jax.ShapeDtypeStruct((n, cap), jnp.int32),
                   jax.ShapeDtypeStruct((n, cap), F32)],
        compiler_params=pltpu.CompilerParams(dimension_semantics=("parallel",)),
        name="topk",
    )(afft)


WEIGHT_SLOTS = 3
SCATTER_GROUP = 8


def _experts_kernel(idx_ref, w_ref, h2_ref, wg_hbm, wu_hbm, wd_hbm, out_ref,
                    xg_a, xg_b, ye_a, ye_b, wbuf, wsem, *, cap, n_exp, n_real):
    g = pl.program_id(0)

    def weight_copies(pair):
        expert = pair % n_exp
        slot = pair % WEIGHT_SLOTS
        return [pltpu.make_async_copy(src.at[expert], wbuf.at[slot, k], wsem.at[slot, k])
                for k, src in enumerate((wg_hbm, wu_hbm, wd_hbm))]

    def gather(pair, dst):
        base = pair * cap
        for c in range(cap):
            dst[c:c + 1, :] = h2_ref[pl.ds(idx_ref[base + c], 1), :]

    @pl.when(g == 0)
    def _():
        for pair in range(WEIGHT_SLOTS - 1):
            for copy in weight_copies(pair):
                copy.start()
        ye_b[...] = jnp.zeros_like(ye_b)
        gather(0, xg_a)

    ahead = g + WEIGHT_SLOTS - 1

    @pl.when(ahead < n_real)
    def _():
        for copy in weight_copies(ahead):
            copy.start()

    @pl.when(g < n_real)
    def _():
        for copy in weight_copies(g):
            copy.wait()

    @pl.when((g == 0) | ((g - 1) % n_exp == 0))
    def _():
        out_ref[...] = jnp.zeros_like(out_ref)

    def step(xg_cur, xg_nxt, ye_cur, ye_prv):
        gather(jnp.minimum(g + 1, n_real - 1), xg_nxt)
        xg = xg_cur[...].astype(BF16)
        slot = g % WEIGHT_SLOTS
        hg = jnp.dot(xg, wbuf[slot, 0], preferred_element_type=F32)
        hu = jnp.dot(xg, wbuf[slot, 1], preferred_element_type=F32)
        hid = (hg * _sigmoid(hg) * hu).astype(BF16)
        ye_cur[...] = jnp.dot(hid, wbuf[slot, 2], preferred_element_type=F32)
        base = jnp.maximum(g - 1, 0) * cap
        for c0 in range(0, cap, SCATTER_GROUP):
            toks = [idx_ref[base + c0 + i] for i in range(SCATTER_GROUP)]
            rows = [out_ref[pl.ds(toks[i], 1), :] + ye_prv[c0 + i:c0 + i + 1, :] * w_ref[base + c0 + i]
                    for i in range(SCATTER_GROUP)]
            for i in range(SCATTER_GROUP):
                out_ref[pl.ds(toks[i], 1), :] = rows[i]

    @pl.when(g % 2 == 0)
    def _():
        step(xg_a, xg_b, ye_a, ye_b)

    @pl.when(g % 2 == 1)
    def _():
        step(xg_b, xg_a, ye_b, ye_a)


def _experts(h2, idx, w, wg, wu, wd, cap):
    b, s, d = h2.shape
    ne, _, f = wg.shape
    n_real = b * ne
    last = n_real - 1
    seq_of = lambda pair: pair // ne
    assert wg.shape == wu.shape == (ne, d, f) and wd.shape == (ne, f, d) and d == f
    return pl.pallas_call(
        functools.partial(_experts_kernel, cap=cap, n_exp=ne, n_real=n_real),
        grid_spec=pltpu.PrefetchScalarGridSpec(
            num_scalar_prefetch=2,
            grid=(n_real + 1,),
            in_specs=[
                pl.BlockSpec((None, s, d), lambda g, idx_, w_: (seq_of(jnp.minimum(g + 1, last)), 0, 0)),
            ] + [pl.BlockSpec(memory_space=pl.ANY)] * 3,
            out_specs=pl.BlockSpec((None, s, d), lambda g, idx_, w_: (seq_of(jnp.maximum(g - 1, 0)), 0, 0)),
            scratch_shapes=[pltpu.VMEM((cap, d), F32)] * 4 + [
                pltpu.VMEM((WEIGHT_SLOTS, 3, d, f), BF16),
                pltpu.SemaphoreType.DMA((WEIGHT_SLOTS, 3)),
            ],
        ),
        out_shape=jax.ShapeDtypeStruct((b, s, d), F32),
        compiler_params=pltpu.CompilerParams(
            dimension_semantics=("arbitrary",), vmem_limit_bytes=EXPERTS_VMEM_LIMIT),
        name="experts",
    )(idx.reshape(-1), w.reshape(-1), h2, wg, wu, wd)


def _final_kernel(x1_ref, moe_ref, p_ref, n3g_ref, wpg_ref, bpg_ref, wple_ref, pg_ref, fg_ref, o_ref,
                  *, last_layer):
    x2 = x1_ref[...] + moe_ref[...]
    hn = _rms(x2, n3g_ref[...]).astype(BF16)
    gate = _sigmoid(jnp.dot(hn, wpg_ref[...], preferred_element_type=F32) + bpg_ref[...])
    emb = jnp.dot(p_ref[...].astype(BF16), wple_ref[...], preferred_element_type=F32)
    x3 = x2 + gate * _rms(emb, pg_ref[...])
    o_ref[...] = _rms(x3, fg_ref[...]) if last_layer else x3


def _final(x1, moe, p2d, n3g, wpg, bpg, wple, pg, fg, tm, last_layer):
    t, d = x1.shape
    q = p2d.shape[1]

    def full(arr):
        nd = arr.ndim
        return pl.BlockSpec(arr.shape, lambda i: (0,) * nd)

    params = (n3g, wpg, bpg, wple, pg, fg)
    return pl.pallas_call(
        functools.partial(_final_kernel, last_layer=last_layer),
        grid=(t // tm,),
        in_specs=[
            pl.BlockSpec((tm, d), lambda i: (i, 0)),
            pl.BlockSpec((tm, d), lambda i: (i, 0)),
            pl.BlockSpec((tm, q), lambda i: (i, 0)),
        ] + [full(a) for a in params],
        out_specs=pl.BlockSpec((tm, d), lambda i: (i, 0)),
        out_shape=jax.ShapeDtypeStruct((t, d), F32),
        compiler_params=pltpu.CompilerParams(
            dimension_semantics=("parallel",), vmem_limit_bytes=VMEM_LIMIT),
        name="final",
    )(x1, moe, p2d, *params)


def _router_pieces(w_router):
    w_hi = w_router.astype(BF16)
    w_lo = (w_router - w_hi.astype(F32)).astype(BF16)
    return jnp.concatenate([w_hi, w_hi, w_lo], axis=0).T


def kernel(x, p, norm1_g, w_in, b_gate, conv_w, conv_b, conv_ln_g, conv_ln_b, w_conv_out, w_pool,
           pool_scale, w_out, norm2_g, w_router, w_exp_gate, w_exp_up, w_exp_down, norm3_g,
           w_ple_gate, b_ple_gate, w_ple, ple_norm_g, final_g):
    b, s, d = x.shape
    depth = w_in.shape[0]
    cap = max(1, CAPACITY_FACTOR * s // N_EXPERTS)
    row = lambda a: a.reshape(1, -1)

    for l in range(depth):
        x1, h2, afft = _mixer(
            x, row(norm1_g[l]), w_in[l].astype(BF16), row(b_gate[l]), conv_w[l], row(conv_b[l]),
            row(conv_ln_g[l]), row(conv_ln_b[l]), w_conv_out[l].astype(BF16),
            w_pool[l].astype(BF16), row(pool_scale[l]), w_out[l].astype(BF16),
            row(norm2_g[l]), _router_pieces(w_router[l]), ts=1024, rc=64, n_chains=4)
        idx, wts = _topk(afft.reshape(b * N_EXPERTS, s), cap, rows_per_step=4 * N_EXPERTS)
        moe = _experts(h2, idx, wts, w_exp_gate[l].astype(BF16), w_exp_up[l].astype(BF16),
                       w_exp_down[l].astype(BF16), cap)
        x = _final(x1.reshape(b * s, d), moe.reshape(b * s, d), p[l].reshape(b * s, -1),
                   row(norm3_g[l]), w_ple_gate[l].astype(BF16), row(b_ple_gate[l]),
                   w_ple[l].astype(BF16), row(ple_norm_g[l]),
                   row(final_g), tm=512, last_layer=l == depth - 1).reshape(b, s, d)
    return x
```

```python
import functools

import jax
import jax.numpy as jnp
from jax import lax
from jax.experimental import pallas as pl
from jax.experimental.pallas import tpu as pltpu

F32 = jnp.float32
BF16 = jnp.bfloat16

CONV_DIM = 512
CONV_WIDTH = 31
CONV_HALF = CONV_WIDTH // 2
POOL_DIM = 512
POOL_WINDOWS = (2, 4, 8, 16)
POOL_GROUP_DIM = POOL_DIM // len(POOL_WINDOWS)
N_EXPERTS = 16
CAPACITY_FACTOR = 2
EPS = 1e-6

SUBLANES = 8
HALO = 16
VMEM_LIMIT = 56 * 1024 * 1024
EXPERTS_VMEM_LIMIT = 60 * 1024 * 1024


def _rms(x, g):
    return x * lax.rsqrt(jnp.mean(x * x, axis=-1, keepdims=True) + EPS) * g


def _sigmoid(x):
    return 1.0 / (1.0 + jnp.exp(-x))


def _mixer_kernel(x_ref, xp_ref, xn_ref,
                  n1g_ref, win_ref, bg_ref, cw_ref, cb_ref, lg_ref, lb_ref, wco_ref,
                  wp_ref, ps_ref, wo_ref, n2g_ref, wrt_ref,
                  ew0_ref, ew1_ref, ew2_ref,
                  x1_ref, h2_ref, afft_ref, eb0_ref, eb1_ref, eb2_ref,
                  hext_ref, *chain_scratch, ts, seq, rc, n_chains):
    i = pl.program_id(1)
    nt = pl.num_programs(1)
    d = x_ref.shape[-1]
    c2 = 2 * CONV_DIM
    c3 = c2 + POOL_DIM
    tc = ts // n_chains
    ext = tc + 2 * HALO

    for src, dst in ((ew0_ref, eb0_ref), (ew1_ref, eb1_ref), (ew2_ref, eb2_ref)):
        dst[...] = src[...].astype(BF16)

    hext_ref[0:HALO, :] = _rms(xp_ref[...], n1g_ref[...]).astype(BF16)
    hext_ref[HALO:HALO + ts, :] = _rms(x_ref[...], n1g_ref[...]).astype(BF16)
    hext_ref[HALO + ts:2 * HALO + ts, :] = _rms(xn_ref[...], n1g_ref[...]).astype(BF16)

    def scratch(j):
        per_chain = len(chain_scratch) // n_chains
        return chain_scratch[j * per_chain:(j + 1) * per_chain]

    def in_proj(j):
        vext_ref, uext_ref, _, _ = scratch(j)
        zc = jnp.dot(hext_ref[j * tc:j * tc + ext, :], win_ref[:, :c3], preferred_element_type=F32)
        vext_ref[...] = zc[:, :CONV_DIM] * _sigmoid(zc[:, CONV_DIM:c2])
        uext_ref[...] = zc[:, c2:]
        if j == 0:
            for ref in (vext_ref, uext_ref):
                ref[0:HALO, :] = jnp.where(i == 0, 0.0, ref[0:HALO, :])
        if j == n_chains - 1:
            for ref in (vext_ref, uext_ref):
                ref[HALO + tc:ext, :] = jnp.where(i == nt - 1, 0.0, ref[HALO + tc:ext, :])

    def gate_logits(j):
        h = hext_ref[HALO + j * tc:HALO + (j + 1) * tc, :]
        return jnp.dot(h, win_ref[:, c3:], preferred_element_type=F32) + bg_ref[...]

    def conv_branch(j):
        vext_ref, _, vsh_ref, conv_ref = scratch(j)
        nsh = vsh_ref.shape[1]
        for ph in range(SUBLANES):
            vsh_ref[ph] = vext_ref[ph:ph + nsh, :]
        for c in range(tc // rc):
            acc = jnp.zeros((rc, CONV_DIM), F32)
            for k in range(CONV_WIDTH):
                off = HALO - CONV_HALF + k
                ph = off % SUBLANES
                r0 = c * rc + off - ph
                acc = acc + vsh_ref[ph, r0:r0 + rc, :] * cw_ref[k:k + 1, :]
            conv_ref[c * rc:(c + 1) * rc, :] = acc + cb_ref[...]
        cv = conv_ref[...]
        mu = jnp.mean(cv, axis=-1, keepdims=True)
        xc = cv - mu
        var = jnp.mean(xc * xc, axis=-1, keepdims=True)
        ln = xc * lax.rsqrt(var + EPS) * lg_ref[...] + lb_ref[...]
        return (ln * _sigmoid(ln)).astype(BF16)

    def merge(j, act, gate_pre):
        _, uext_ref, _, _ = scratch(j)
        rows = slice(j * tc, (j + 1) * tc)
        y_conv = jnp.dot(act, wco_ref[...], preferred_element_type=F32)

        pos = i * ts + j * tc + lax.broadcasted_iota(jnp.int32, (tc, 1), 0)
        pooled = []
        for g, w in enumerate(POOL_WINDOWS):
            lanes = slice(g * POOL_GROUP_DIM, (g + 1) * POOL_GROUP_DIM)
            ue = uext_ref[:, lanes]
            run, width = ue, 1
            while width < w:
                run = run + pltpu.roll(run, width, axis=0)
                width *= 2
            ahead = (w - w // 2) - 1
            if ahead:
                run = pltpu.roll(run, ext - ahead, axis=0)
            lo = jnp.maximum(pos - w // 2, 0)
            hi = jnp.minimum(pos + ahead, seq - 1)
            cnt = (hi - lo + 1).astype(F32)
            dgrp = run[HALO:HALO + tc] / cnt - ue[HALO:HALO + tc]
            pooled.append(jnp.dot(dgrp.astype(BF16), wp_ref[g], preferred_element_type=F32))
        y_pool = jnp.concatenate(pooled, axis=-1) * ps_ref[...]

        gates = _sigmoid(gate_pre)
        merged = gates[:, :d] * y_conv + gates[:, d:] * y_pool
        x1 = x_ref[rows, :] + jnp.dot(merged.astype(BF16), wo_ref[...], preferred_element_type=F32)
        x1_ref[rows, :] = x1

        h2 = _rms(x1, n2g_ref[...])
        h2_ref[rows, :] = h2
        h2_hi = h2.astype(BF16)
        h2_lo = (h2 - h2_hi.astype(F32)).astype(BF16)
        logits_t = lax.dot_general(wrt_ref[...], jnp.concatenate([h2_hi, h2_lo, h2_hi], axis=-1),
                                   (((1,), (1,)), ((), ())), preferred_element_type=F32)
        ex = jnp.exp(logits_t - jnp.max(logits_t, axis=0, keepdims=True))
        afft_ref[:, rows] = ex / jnp.sum(ex, axis=0, keepdims=True)

    in_proj(0)
    pending = None
    for j in range(n_chains):
        if j + 1 < n_chains:
            in_proj(j + 1)
        gate_pre = gate_logits(j)
        act = conv_branch(j)
        if pending is not None:
            merge(*pending)
        pending = (j, act, gate_pre)
    merge(*pending)


def _mixer(x, n1g, win, bg, cw, cb, lg, lb, wco, wp, ps, wo, n2g, wrt, expert_w, ts, rc, n_chains):
    b, s, d = x.shape
    nt = s // ts
    flat_w = [w.reshape(-1, w.shape[-1]) for w in expert_w]
    share = flat_w[0].shape[0] // (b * nt)
    assert all(w.shape == flat_w[0].shape for w in flat_w) and share * b * nt == flat_w[0].shape[0]
    share_spec = pl.BlockSpec((share, flat_w[0].shape[1]), lambda bi, i: (bi * nt + i, 0))
    hb = ts // HALO
    nhb = s // HALO
    tc = ts // n_chains

    def cur(bi, i):
        return (bi, i, 0)

    def prev(bi, i):
        return (bi, jnp.maximum(i * hb - 1, 0), 0)

    def nxt(bi, i):
        return (bi, jnp.minimum((i + 1) * hb, nhb - 1), 0)

    def full(arr):
        nd = arr.ndim
        return pl.BlockSpec(arr.shape, lambda bi, i: (0,) * nd, pipeline_mode=pl.Buffered(1))

    params = (n1g, win, bg, cw, cb, lg, lb, wco, wp, ps, wo, n2g, wrt)
    chain_scratch = [
        pltpu.VMEM((tc + 2 * HALO, CONV_DIM), F32),
        pltpu.VMEM((tc + 2 * HALO, POOL_DIM), F32),
        pltpu.VMEM((SUBLANES, tc + 2 * HALO - SUBLANES, CONV_DIM), F32),
        pltpu.VMEM((tc, CONV_DIM), F32),
    ]
    return pl.pallas_call(
        functools.partial(_mixer_kernel, ts=ts, seq=s, rc=rc, n_chains=n_chains),
        grid=(b, nt),
        in_specs=[
            pl.BlockSpec((None, ts, d), cur),
            pl.BlockSpec((None, HALO, d), prev),
            pl.BlockSpec((None, HALO, d), nxt),
        ] + [full(a) for a in params] + [share_spec] * 3,
        out_specs=[
            pl.BlockSpec((None, ts, d), cur),
            pl.BlockSpec((None, ts, d), cur),
            pl.BlockSpec((None, N_EXPERTS, ts), lambda bi, i: (bi, 0, i)),
        ] + [share_spec] * 3,
        out_shape=[
            jax.ShapeDtypeStruct((b, s, d), F32),
            jax.ShapeDtypeStruct((b, s, d), F32),
            jax.ShapeDtypeStruct((b, N_EXPERTS, s), F32),
        ] + [jax.ShapeDtypeStruct(w.shape, BF16) for w in flat_w],
        scratch_shapes=[pltpu.VMEM((ts + 2 * HALO, d), BF16)] + chain_scratch * n_chains,
        compiler_params=pltpu.CompilerParams(
            dimension_semantics=("parallel", "parallel"), vmem_limit_bytes=VMEM_LIMIT),
        name="mixer",
    )(x, x, x, *params, *flat_w)


def _cumsum_lanes(x):
    n = x.shape[-1]
    lane = lax.broadcasted_iota(jnp.int32, x.shape, x.ndim - 1)
    step = 1
    while step < n:
        x = x + jnp.where(lane >= step, pltpu.roll(x, step, axis=x.ndim - 1), 0)
        step *= 2
    return x


NO_TOKEN = 1 << 20


def _topk_kernel(a_ref, idx_ref, w_ref, *, cap):
    a = a_ref[...]
    as_float = lambda word: pltpu.bitcast(word, F32)
    thr = jnp.zeros((a.shape[0], 1), jnp.int32)
    for bit in range(30, -1, -1):
        cand = thr | (1 << bit)
        cnt = jnp.sum((a >= as_float(cand)).astype(jnp.int32), axis=1, keepdims=True)
        thr = jnp.where(cnt >= cap, cand, thr)
    gt = a >= as_float(thr + 1)
    eq = jnp.where(gt, 0, (a >= as_float(thr)).astype(jnp.int32))
    need = cap - jnp.sum(gt.astype(jnp.int32), axis=1, keepdims=True)
    eq_rank = _cumsum_lanes(eq) - eq
    sel = jnp.where(gt, 1, jnp.where(eq_rank < need, eq, 0))
    slot = _cumsum_lanes(sel) - sel

    s = a.shape[1]
    lane = lax.broadcasted_iota(jnp.int32, a.shape, 1)
    from_right = lambda x, k: pltpu.roll(x, s - k, axis=1)
    tok, wv = lane, a
    owed = jnp.where(sel > 0, lane - slot, NO_TOKEN)
    k = 1
    while k < s:
        in_owed = from_right(owed, k)
        incoming = (in_owed & k) != 0
        tok = jnp.where(incoming, from_right(tok, k), tok)
        wv = jnp.where(incoming, from_right(wv, k), wv)
        owed = jnp.where(incoming, in_owed ^ k, jnp.where((owed & k) != 0, NO_TOKEN, owed))
        k *= 2
    idx_ref[...] = tok[:, :cap]
    w_ref[...] = wv[:, :cap]


def _topk(afft, cap, rows_per_step):
    n, s = afft.shape
    return pl.pallas_call(
        functools.partial(_topk_kernel, cap=cap),
        grid=(n // rows_per_step,),
        in_specs=[pl.BlockSpec((rows_per_step, s), lambda i: (i, 0))],
        out_specs=[pl.BlockSpec((rows_per_step, cap), lambda i: (i, 0)),
                   pl.BlockSpec((rows_per_step, cap), lambda i: (i, 0))],
        out_shape=[jax.ShapeDtypeStruct((n, cap), jnp.int32),
                   jax.ShapeDtypeStruct((n, cap), F32)],
        compiler_params=pltpu.CompilerParams(dimension_semantics=("parallel",)),
        name="topk",
    )(afft)


WEIGHT_SLOTS = 3
SCATTER_GROUP = 8


def _experts_kernel(idx_ref, w_ref, h2_ref, wg_hbm, wu_hbm, wd_hbm, out_ref,
                    xg_a, xg_b, ye_a, ye_b, wbuf, wsem, *, cap, n_exp, n_real):
    g = pl.program_id(0)

    def weight_copies(pair):
        expert = pair % n_exp
        slot = pair % WEIGHT_SLOTS
        return [pltpu.make_async_copy(src.at[expert], wbuf.at[slot, k], wsem.at[slot, k])
                for k, src in enumerate((wg_hbm, wu_hbm, wd_hbm))]

    def gather(pair, dst):
        base = pair * cap
        for c in range(cap):
            dst[c:c + 1, :] = h2_ref[pl.ds(idx_ref[base + c], 1), :]

    @pl.when(g == 0)
    def _():
        for pair in range(WEIGHT_SLOTS - 1):
            for copy in weight_copies(pair):
                copy.start()
        ye_b[...] = jnp.zeros_like(ye_b)
        gather(0, xg_a)

    ahead = g + WEIGHT_SLOTS - 1

    @pl.when(ahead < n_real)
    def _():
        for copy in weight_copies(ahead):
            copy.start()

    @pl.when(g < n_real)
    def _():
        for copy in weight_copies(g):
            copy.wait()

    @pl.when((g == 0) | ((g - 1) % n_exp == 0))
    def _():
        out_ref[...] = jnp.zeros_like(out_ref)

    def step(xg_cur, xg_nxt, ye_cur, ye_prv):
        gather(jnp.minimum(g + 1, n_real - 1), xg_nxt)
        xg = xg_cur[...].astype(BF16)
        slot = g % WEIGHT_SLOTS
        hg = jnp.dot(xg, wbuf[slot, 0], preferred_element_type=F32)
        hu = jnp.dot(xg, wbuf[slot, 1], preferred_element_type=F32)
        hid = (hg * _sigmoid(hg) * hu).astype(BF16)
        ye_cur[...] = jnp.dot(hid, wbuf[slot, 2], preferred_element_type=F32)
        base = jnp.maximum(g - 1, 0) * cap
        for c0 in range(0, cap, SCATTER_GROUP):
            toks = [idx_ref[base + c0 + i] for i in range(SCATTER_GROUP)]
            rows = [out_ref[pl.ds(toks[i], 1), :] + ye_prv[c0 + i:c0 + i + 1, :] * w_ref[base + c0 + i]
                    for i in range(SCATTER_GROUP)]
            for i in range(SCATTER_GROUP):
                out_ref[pl.ds(toks[i], 1), :] = rows[i]

    @pl.when(g % 2 == 0)
    def _():
        step(xg_a, xg_b, ye_a, ye_b)

    @pl.when(g % 2 == 1)
    def _():
        step(xg_b, xg_a, ye_b, ye_a)


def _experts(h2, idx, w, wg, wu, wd, cap):
    b, s, d = h2.shape
    ne, _, f = wg.shape
    n_real = b * ne
    last = n_real - 1
    seq_of = lambda pair: pair // ne
    assert wg.shape == wu.shape == (ne, d, f) and wd.shape == (ne, f, d) and d == f
    return pl.pallas_call(
        functools.partial(_experts_kernel, cap=cap, n_exp=ne, n_real=n_real),
        grid_spec=pltpu.PrefetchScalarGridSpec(
            num_scalar_prefetch=2,
            grid=(n_real + 1,),
            in_specs=[
                pl.BlockSpec((None, s, d), lambda g, idx_, w_: (seq_of(jnp.minimum(g + 1, last)), 0, 0)),
            ] + [pl.BlockSpec(memory_space=pl.ANY)] * 3,
            out_specs=pl.BlockSpec((None, s, d), lambda g, idx_, w_: (seq_of(jnp.maximum(g - 1, 0)), 0, 0)),
            scratch_shapes=[pltpu.VMEM((cap, d), F32)] * 4 + [
                pltpu.VMEM((WEIGHT_SLOTS, 3, d, f), BF16),
                pltpu.SemaphoreType.DMA((WEIGHT_SLOTS, 3)),
            ],
        ),
        out_shape=jax.ShapeDtypeStruct((b, s, d), F32),
        compiler_params=pltpu.CompilerParams(
            dimension_semantics=("arbitrary",), vmem_limit_bytes=EXPERTS_VMEM_LIMIT),
        name="experts",
    )(idx.reshape(-1), w.reshape(-1), h2, wg, wu, wd)


def _final_kernel(x1_ref, moe_ref, p_ref, n3g_ref, wpg_ref, bpg_ref, wple_ref, pg_ref, fg_ref, o_ref,
                  *, last_layer, n_chains):
    tc = o_ref.shape[0] // n_chains
    for j in range(n_chains):
        rows = slice(j * tc, (j + 1) * tc)
        x2 = x1_ref[rows, :] + moe_ref[rows, :]
        hn = _rms(x2, n3g_ref[...]).astype(BF16)
        gate = _sigmoid(jnp.dot(hn, wpg_ref[...], preferred_element_type=F32) + bpg_ref[...])
        emb = jnp.dot(p_ref[rows, :].astype(BF16), wple_ref[...], preferred_element_type=F32)
        x3 = x2 + gate * _rms(emb, pg_ref[...])
        o_ref[rows, :] = _rms(x3, fg_ref[...]) if last_layer else x3


def _final(x1, moe, p2d, n3g, wpg, bpg, wple, pg, fg, tm, n_chains, last_layer):
    t, d = x1.shape
    q = p2d.shape[1]

    def full(arr):
        nd = arr.ndim
        return pl.BlockSpec(arr.shape, lambda i: (0,) * nd)

    params = (n3g, wpg, bpg, wple, pg, fg)
    return pl.pallas_call(
        functools.partial(_final_kernel, last_layer=last_layer, n_chains=n_chains),
        grid=(t // tm,),
        in_specs=[
            pl.BlockSpec((tm, d), lambda i: (i, 0)),
            pl.BlockSpec((tm, d), lambda i: (i, 0)),
            pl.BlockSpec((tm, q), lambda i: (i, 0)),
        ] + [full(a) for a in params],
        out_specs=pl.BlockSpec((tm, d), lambda i: (i, 0)),
        out_shape=jax.ShapeDtypeStruct((t, d), F32),
        compiler_params=pltpu.CompilerParams(
            dimension_semantics=("parallel",), vmem_limit_bytes=VMEM_LIMIT),
        name="final",
    )(x1, moe, p2d, *params)


def _router_pieces(w_router):
    w_hi = w_router.astype(BF16)
    w_lo = (w_router - w_hi.astype(F32)).astype(BF16)
    return jnp.concatenate([w_hi, w_hi, w_lo], axis=0).T


def kernel(x, p, norm1_g, w_in, b_gate, conv_w, conv_b, conv_ln_g, conv_ln_b, w_conv_out, w_pool,
           pool_scale, w_out, norm2_g, w_router, w_exp_gate, w_exp_up, w_exp_down, norm3_g,
           w_ple_gate, b_ple_gate, w_ple, ple_norm_g, final_g):
    b, s, d = x.shape
    depth = w_in.shape[0]
    cap = max(1, CAPACITY_FACTOR * s // N_EXPERTS)
    row = lambda a: a.reshape(1, -1)

    for l in range(depth):
        expert_w = (w_exp_gate[l], w_exp_up[l], w_exp_down[l])
        x1, h2, afft, *expert_b = _mixer(
            x, row(norm1_g[l]), w_in[l].astype(BF16), row(b_gate[l]), conv_w[l], row(conv_b[l]),
            row(conv_ln_g[l]), row(conv_ln_b[l]), w_conv_out[l].astype(BF16),
            w_pool[l].astype(BF16), row(pool_scale[l]), w_out[l].astype(BF16),
            row(norm2_g[l]), _router_pieces(w_router[l]), expert_w, ts=512, rc=64, n_chains=2)
        idx, wts = _topk(afft.reshape(b * N_EXPERTS, s), cap, rows_per_step=4 * N_EXPERTS)
        moe = _experts(h2, idx, wts, *(wb.reshape(w.shape) for wb, w in zip(expert_b, expert_w)), cap)
        x = _final(x1.reshape(b * s, d), moe.reshape(b * s, d), p[l].reshape(b * s, -1),
                   row(norm3_g[l]), w_ple_gate[l].astype(BF16), row(b_ple_gate[l]),
                   w_ple[l].astype(BF16), row(ple_norm_g[l]),
                   row(final_g), tm=1024, n_chains=4, last_layer=l == depth - 1).reshape(b, s, d)
    return x
```

```python
import functools

import jax
import jax.numpy as jnp
from jax import lax
from jax.experimental import pallas as pl
from jax.experimental.pallas import tpu as pltpu

F32 = jnp.float32
BF16 = jnp.bfloat16

CONV_DIM = 512
CONV_WIDTH = 31
CONV_HALF = CONV_WIDTH // 2
POOL_DIM = 512
POOL_WINDOWS = (2, 4, 8, 16)
POOL_GROUP_DIM = POOL_DIM // len(POOL_WINDOWS)
N_EXPERTS = 16
CAPACITY_FACTOR = 2
EPS = 1e-6

SUBLANES = 8
HALO = 16
VMEM_LIMIT = 56 * 1024 * 1024
EXPERTS_VMEM_LIMIT = 60 * 1024 * 1024


def _rms(x, g):
    return x * lax.rsqrt(jnp.mean(x * x, axis=-1, keepdims=True) + EPS) * g


def _sigmoid(x):
    return 1.0 / (1.0 + jnp.exp(-x))


def _mixer_kernel(x_ref, xp_ref, xn_ref,
                  n1g_ref, win_ref, bg_ref, cw_ref, cb_ref, lg_ref, lb_ref, wco_ref,
                  wp_ref, ps_ref, wo_ref, n2g_ref, wrt_ref,
                  ew0_ref, ew1_ref, ew2_ref,
                  x1_ref, h2_ref, afft_ref, eb0_ref, eb1_ref, eb2_ref,
                  hext_ref, *chain_scratch, ts, seq, rc, n_chains):
    i = pl.program_id(1)
    nt = pl.num_programs(1)
    d = x_ref.shape[-1]
    c2 = 2 * CONV_DIM
    c3 = c2 + POOL_DIM
    tc = ts // n_chains
    ext = tc + 2 * HALO

    for src, dst in ((ew0_ref, eb0_ref), (ew1_ref, eb1_ref), (ew2_ref, eb2_ref)):
        dst[...] = src[...].astype(BF16)

    hext_ref[0:HALO, :] = _rms(xp_ref[...], n1g_ref[...]).astype(BF16)
    hext_ref[HALO:HALO + ts, :] = _rms(x_ref[...], n1g_ref[...]).astype(BF16)
    hext_ref[HALO + ts:2 * HALO + ts, :] = _rms(xn_ref[...], n1g_ref[...]).astype(BF16)

    def scratch(j):
        per_chain = len(chain_scratch) // n_chains
        return chain_scratch[j * per_chain:(j + 1) * per_chain]

    def in_proj(j):
        vext_ref, uext_ref, _, _ = scratch(j)
        zc = jnp.dot(hext_ref[j * tc:j * tc + ext, :], win_ref[:, :c3], preferred_element_type=F32)
        vext_ref[...] = zc[:, :CONV_DIM] * _sigmoid(zc[:, CONV_DIM:c2])
        uext_ref[...] = zc[:, c2:]
        if j == 0:
            for ref in (vext_ref, uext_ref):
                ref[0:HALO, :] = jnp.where(i == 0, 0.0, ref[0:HALO, :])
        if j == n_chains - 1:
            for ref in (vext_ref, uext_ref):
                ref[HALO + tc:ext, :] = jnp.where(i == nt - 1, 0.0, ref[HALO + tc:ext, :])

    def gate_logits(j):
        h = hext_ref[HALO + j * tc:HALO + (j + 1) * tc, :]
        return jnp.dot(h, win_ref[:, c3:], preferred_element_type=F32) + bg_ref[...]

    def conv_branch(j):
        vext_ref, _, vsh_ref, conv_ref = scratch(j)
        nsh = vsh_ref.shape[1]
        for ph in range(SUBLANES):
            vsh_ref[ph] = vext_ref[ph:ph + nsh, :]
        for c in range(tc // rc):
            acc = jnp.zeros((rc, CONV_DIM), F32)
            for k in range(CONV_WIDTH):
                off = HALO - CONV_HALF + k
                ph = off % SUBLANES
                r0 = c * rc + off - ph
                acc = acc + vsh_ref[ph, r0:r0 + rc, :] * cw_ref[k:k + 1, :]
            conv_ref[c * rc:(c + 1) * rc, :] = acc + cb_ref[...]
        cv = conv_ref[...]
        mu = jnp.mean(cv, axis=-1, keepdims=True)
        xc = cv - mu
        var = jnp.mean(xc * xc, axis=-1, keepdims=True)
        ln = xc * lax.rsqrt(var + EPS) * lg_ref[...] + lb_ref[...]
        return (ln * _sigmoid(ln)).astype(BF16)

    def merge(j, act, gate_pre):
        _, uext_ref, _, _ = scratch(j)
        rows = slice(j * tc, (j + 1) * tc)
        y_conv = jnp.dot(act, wco_ref[...], preferred_element_type=F32)

        pos = i * ts + j * tc + lax.broadcasted_iota(jnp.int32, (tc, 1), 0)
        pooled = []
        for g, w in enumerate(POOL_WINDOWS):
            lanes = slice(g * POOL_GROUP_DIM, (g + 1) * POOL_GROUP_DIM)
            ue = uext_ref[:, lanes]
            run, width = ue, 1
            while width < w:
                run = run + pltpu.roll(run, width, axis=0)
                width *= 2
            ahead = (w - w // 2) - 1
            if ahead:
                run = pltpu.roll(run, ext - ahead, axis=0)
            lo = jnp.maximum(pos - w // 2, 0)
            hi = jnp.minimum(pos + ahead, seq - 1)
            cnt = (hi - lo + 1).astype(F32)
            dgrp = run[HALO:HALO + tc] / cnt - ue[HALO:HALO + tc]
            pooled.append(jnp.dot(dgrp.astype(BF16), wp_ref[g], preferred_element_type=F32))
        y_pool = jnp.concatenate(pooled, axis=-1) * ps_ref[...]

        gates = _sigmoid(gate_pre)
        merged = gates[:, :d] * y_conv + gates[:, d:] * y_pool
        x1 = x_ref[rows, :] + jnp.dot(merged.astype(BF16), wo_ref[...], preferred_element_type=F32)
        x1_ref[rows, :] = x1

        h2 = _rms(x1, n2g_ref[...])
        h2_ref[rows, :] = h2
        h2_hi = h2.astype(BF16)
        h2_lo = (h2 - h2_hi.astype(F32)).astype(BF16)
        logits_t = lax.dot_general(wrt_ref[...], jnp.concatenate([h2_hi, h2_lo, h2_hi], axis=-1),
                                   (((1,), (1,)), ((), ())), preferred_element_type=F32)
        ex = jnp.exp(logits_t - jnp.max(logits_t, axis=0, keepdims=True))
        afft_ref[:, rows] = ex / jnp.sum(ex, axis=0, keepdims=True)

    in_proj(0)
    pending = None
    for j in range(n_chains):
        if j + 1 < n_chains:
            in_proj(j + 1)
        gate_pre = gate_logits(j)
        act = conv_branch(j)
        if pending is not None:
            merge(*pending)
        pending = (j, act, gate_pre)
    merge(*pending)


def _mixer(x, n1g, win, bg, cw, cb, lg, lb, wco, wp, ps, wo, n2g, wrt, expert_w, ts, rc, n_chains):
    b, s, d = x.shape
    nt = s // ts
    flat_w = [w.reshape(-1, w.shape[-1]) for w in expert_w]
    share = flat_w[0].shape[0] // (b * nt)
    assert all(w.shape == flat_w[0].shape for w in flat_w) and share * b * nt == flat_w[0].shape[0]
    share_spec = pl.BlockSpec((share, flat_w[0].shape[1]), lambda bi, i: (bi * nt + i, 0))
    hb = ts // HALO
    nhb = s // HALO
    tc = ts // n_chains

    def cur(bi, i):
        return (bi, i, 0)

    def prev(bi, i):
        return (bi, jnp.maximum(i * hb - 1, 0), 0)

    def nxt(bi, i):
        return (bi, jnp.minimum((i + 1) * hb, nhb - 1), 0)

    def full(arr):
        nd = arr.ndim
        return pl.BlockSpec(arr.shape, lambda bi, i: (0,) * nd, pipeline_mode=pl.Buffered(1))

    params = (n1g, win, bg, cw, cb, lg, lb, wco, wp, ps, wo, n2g, wrt)
    chain_scratch = [
        pltpu.VMEM((tc + 2 * HALO, CONV_DIM), F32),
        pltpu.VMEM((tc + 2 * HALO, POOL_DIM), F32),
        pltpu.VMEM((SUBLANES, tc + 2 * HALO - SUBLANES, CONV_DIM), F32),
        pltpu.VMEM((tc, CONV_DIM), F32),
    ]
    return pl.pallas_call(
        functools.partial(_mixer_kernel, ts=ts, seq=s, rc=rc, n_chains=n_chains),
        grid=(b, nt),
        in_specs=[
            pl.BlockSpec((None, ts, d), cur),
            pl.BlockSpec((None, HALO, d), prev),
            pl.BlockSpec((None, HALO, d), nxt),
        ] + [full(a) for a in params] + [share_spec] * 3,
        out_specs=[
            pl.BlockSpec((None, ts, d), cur),
            pl.BlockSpec((None, ts, d), cur),
            pl.BlockSpec((None, N_EXPERTS, ts), lambda bi, i: (bi, 0, i)),
        ] + [share_spec] * 3,
        out_shape=[
            jax.ShapeDtypeStruct((b, s, d), F32),
            jax.ShapeDtypeStruct((b, s, d), F32),
            jax.ShapeDtypeStruct((b, N_EXPERTS, s), F32),
        ] + [jax.ShapeDtypeStruct(w.shape, BF16) for w in flat_w],
        scratch_shapes=[pltpu.VMEM((ts + 2 * HALO, d), BF16)] + chain_scratch * n_chains,
        compiler_params=pltpu.CompilerParams(
            dimension_semantics=("parallel", "parallel"), vmem_limit_bytes=VMEM_LIMIT),
        name="mixer",
    )(x, x, x, *params, *flat_w)


def _cumsum_lanes(x):
    n = x.shape[-1]
    lane = lax.broadcasted_iota(jnp.int32, x.shape, x.ndim - 1)
    step = 1
    while step < n:
        x = x + jnp.where(lane >= step, pltpu.roll(x, step, axis=x.ndim - 1), 0)
        step *= 2
    return x


NO_TOKEN = 1 << 20


def _topk_kernel(a_ref, idx_ref, w_ref, *, cap):
    a = a_ref[...]
    as_float = lambda word: pltpu.bitcast(word, F32)
    thr = jnp.zeros((a.shape[0], 1), jnp.int32)
    for bit in range(30, -1, -1):
        cand = thr | (1 << bit)
        cnt = jnp.sum((a >= as_float(cand)).astype(jnp.int32), axis=1, keepdims=True)
        thr = jnp.where(cnt >= cap, cand, thr)
    gt = a >= as_float(thr + 1)
    eq = jnp.where(gt, 0, (a >= as_float(thr)).astype(jnp.int32))
    need = cap - jnp.sum(gt.astype(jnp.int32), axis=1, keepdims=True)
    eq_rank = _cumsum_lanes(eq) - eq
    sel = jnp.where(gt, 1, jnp.where(eq_rank < need, eq, 0))
    slot = _cumsum_lanes(sel) - sel

    s = a.shape[1]
    lane = lax.broadcasted_iota(jnp.int32, a.shape, 1)
    from_right = lambda x, k: pltpu.roll(x, s - k, axis=1)
    tok, wv = lane, a
    owed = jnp.where(sel > 0, lane - slot, NO_TOKEN)
    k = 1
    while k < s:
        in_owed = from_right(owed, k)
        incoming = (in_owed & k) != 0
        tok = jnp.where(incoming, from_right(tok, k), tok)
        wv = jnp.where(incoming, from_right(wv, k), wv)
        owed = jnp.where(incoming, in_owed ^ k, jnp.where((owed & k) != 0, NO_TOKEN, owed))
        k *= 2
    idx_ref[...] = tok[:, :cap]
    w_ref[...] = wv[:, :cap]


def _topk(afft, cap, rows_per_step):
    n, s = afft.shape
    return pl.pallas_call(
        functools.partial(_topk_kernel, cap=cap),
        grid=(n // rows_per_step,),
        in_specs=[pl.BlockSpec((rows_per_step, s), lambda i: (i, 0))],
        out_specs=[pl.BlockSpec((rows_per_step, cap), lambda i: (i, 0)),
                   pl.BlockSpec((rows_per_step, cap), lambda i: (i, 0))],
        out_shape=[jax.ShapeDtypeStruct((n, cap), jnp.int32),
                   jax.ShapeDtypeStruct((n, cap), F32)],
        compiler_params=pltpu.CompilerParams(dimension_semantics=("parallel",)),
        name="topk",
    )(afft)


WEIGHT_SLOTS = 3
SCATTER_GROUP = 4


def _experts_kernel(idx_ref, w_ref, h2_ref, wg_hbm, wu_hbm, wd_hbm, out_ref,
                    xg_a, xg_b, ye_a, ye_b, wbuf, wsem, *, cap, n_exp, n_real):
    g = pl.program_id(0)

    def weight_copies(pair):
        expert = pair % n_exp
        slot = pair % WEIGHT_SLOTS
        return [pltpu.make_async_copy(src.at[expert], wbuf.at[slot, k], wsem.at[slot, k])
                for k, src in enumerate((wg_hbm, wu_hbm, wd_hbm))]

    def gather(pair, dst):
        base = pair * cap
        for c in range(cap):
            dst[c:c + 1, :] = h2_ref[pl.ds(idx_ref[base + c], 1), :]

    @pl.when(g == 0)
    def _():
        for pair in range(WEIGHT_SLOTS - 1):
            for copy in weight_copies(pair):
                copy.start()
        ye_b[...] = jnp.zeros_like(ye_b)
        gather(0, xg_a)

    ahead = g + WEIGHT_SLOTS - 1

    @pl.when(ahead < n_real)
    def _():
        for copy in weight_copies(ahead):
            copy.start()

    @pl.when(g < n_real)
    def _():
        for copy in weight_copies(g):
            copy.wait()

    @pl.when((g == 0) | ((g - 1) % n_exp == 0))
    def _():
        out_ref[...] = jnp.zeros_like(out_ref)

    def step(xg_cur, xg_nxt, ye_cur, ye_prv):
        gather(jnp.minimum(g + 1, n_real - 1), xg_nxt)
        xg = xg_cur[...].astype(BF16)
        slot = g % WEIGHT_SLOTS
        hg = jnp.dot(xg, wbuf[slot, 0], preferred_element_type=F32)
        hu = jnp.dot(xg, wbuf[slot, 1], preferred_element_type=F32)
        hid = (hg * _sigmoid(hg) * hu).astype(BF16)
        ye_cur[...] = jnp.dot(hid, wbuf[slot, 2], preferred_element_type=F32)
        base = jnp.maximum(g - 1, 0) * cap
        for c0 in range(0, cap, SCATTER_GROUP):
            toks = [idx_ref[base + c0 + i] for i in range(SCATTER_GROUP)]
            rows = [out_ref[pl.ds(toks[i], 1), :] + ye_prv[c0 + i:c0 + i + 1, :] * w_ref[base + c0 + i]
                    for i in range(SCATTER_GROUP)]
            for i in range(SCATTER_GROUP):
                out_ref[pl.ds(toks[i], 1), :] = rows[i]

    @pl.when(g % 2 == 0)
    def _():
        step(xg_a, xg_b, ye_a, ye_b)

    @pl.when(g % 2 == 1)
    def _():
        step(xg_b, xg_a, ye_b, ye_a)


def _experts(h2, idx, w, wg, wu, wd, cap):
    b, s, d = h2.shape
    ne, _, f = wg.shape
    n_real = b * ne
    last = n_real - 1
    seq_of = lambda pair: pair // ne
    assert wg.shape == wu.shape == (ne, d, f) and wd.shape == (ne, f, d) and d == f
    return pl.pallas_call(
        functools.partial(_experts_kernel, cap=cap, n_exp=ne, n_real=n_real),
        grid_spec=pltpu.PrefetchScalarGridSpec(
            num_scalar_prefetch=2,
            grid=(n_real + 1,),
            in_specs=[
                pl.BlockSpec((None, s, d), lambda g, idx_, w_: (seq_of(jnp.minimum(g + 1, last)), 0, 0)),
            ] + [pl.BlockSpec(memory_space=pl.ANY)] * 3,
            out_specs=pl.BlockSpec((None, s, d), lambda g, idx_, w_: (seq_of(jnp.maximum(g - 1, 0)), 0, 0)),
            scratch_shapes=[pltpu.VMEM((cap, d), F32)] * 4 + [
                pltpu.VMEM((WEIGHT_SLOTS, 3, d, f), BF16),
                pltpu.SemaphoreType.DMA((WEIGHT_SLOTS, 3)),
            ],
        ),
        out_shape=jax.ShapeDtypeStruct((b, s, d), F32),
        compiler_params=pltpu.CompilerParams(
            dimension_semantics=("arbitrary",), vmem_limit_bytes=EXPERTS_VMEM_LIMIT),
        name="experts",
    )(idx.reshape(-1), w.reshape(-1), h2, wg, wu, wd)


def _final_kernel(x1_ref, moe_ref, p_ref, n3g_ref, wpg_ref, bpg_ref, wple_ref, pg_ref, fg_ref, o_ref,
                  *, last_layer, n_chains):
    tc = o_ref.shape[0] // n_chains
    for j in range(n_chains):
        rows = slice(j * tc, (j + 1) * tc)
        x2 = x1_ref[rows, :] + moe_ref[rows, :]
        hn = _rms(x2, n3g_ref[...]).astype(BF16)
        gate = _sigmoid(jnp.dot(hn, wpg_ref[...], preferred_element_type=F32) + bpg_ref[...])
        emb = jnp.dot(p_ref[rows, :].astype(BF16), wple_ref[...], preferred_element_type=F32)
        x3 = x2 + gate * _rms(emb, pg_ref[...])
        o_ref[rows, :] = _rms(x3, fg_ref[...]) if last_layer else x3


def _final(x1, moe, p2d, n3g, wpg, bpg, wple, pg, fg, tm, n_chains, last_layer):
    t, d = x1.shape
    q = p2d.shape[1]

    def full(arr):
        nd = arr.ndim
        return pl.BlockSpec(arr.shape, lambda i: (0,) * nd)

    params = (n3g, wpg, bpg, wple, pg, fg)
    return pl.pallas_call(
        functools.partial(_final_kernel, last_layer=last_layer, n_chains=n_chains),
        grid=(t // tm,),
        in_specs=[
            pl.BlockSpec((tm, d), lambda i: (i, 0)),
            pl.BlockSpec((tm, d), lambda i: (i, 0)),
            pl.BlockSpec((tm, q), lambda i: (i, 0)),
        ] + [full(a) for a in params],
        out_specs=pl.BlockSpec((tm, d), lambda i: (i, 0)),
        out_shape=jax.ShapeDtypeStruct((t, d), F32),
        compiler_params=pltpu.CompilerParams(
            dimension_semantics=("parallel",), vmem_limit_bytes=VMEM_LIMIT),
        name="final",
    )(x1, moe, p2d, *params)


def _router_pieces(w_router):
    w_hi = w_router.astype(BF16)
    w_lo = (w_router - w_hi.astype(F32)).astype(BF16)
    return jnp.concatenate([w_hi, w_hi, w_lo], axis=0).T


def kernel(x, p, norm1_g, w_in, b_gate, conv_w, conv_b, conv_ln_g, conv_ln_b, w_conv_out, w_pool,
           pool_scale, w_out, norm2_g, w_router, w_exp_gate, w_exp_up, w_exp_down, norm3_g,
           w_ple_gate, b_ple_gate, w_ple, ple_norm_g, final_g):
    b, s, d = x.shape
    depth = w_in.shape[0]
    cap = max(1, CAPACITY_FACTOR * s // N_EXPERTS)
    row = lambda a: a.reshape(1, -1)

    for l in range(depth):
        expert_w = (w_exp_gate[l], w_exp_up[l], w_exp_down[l])
        x1, h2, afft, *expert_b = _mixer(
            x, row(norm1_g[l]), w_in[l].astype(BF16), row(b_gate[l]), conv_w[l], row(conv_b[l]),
            row(conv_ln_g[l]), row(conv_ln_b[l]), w_conv_out[l].astype(BF16),
            w_pool[l].astype(BF16), row(pool_scale[l]), w_out[l].astype(BF16),
            row(norm2_g[l]), _router_pieces(w_router[l]), expert_w, ts=512, rc=128, n_chains=2)
        idx, wts = _topk(afft.reshape(b * N_EXPERTS, s), cap, rows_per_step=4 * N_EXPERTS)
        moe = _experts(h2, idx, wts, *(wb.reshape(w.shape) for wb, w in zip(expert_b, expert_w)), cap)
        x = _final(x1.reshape(b * s, d), moe.reshape(b * s, d), p[l].reshape(b * s, -1),
                   row(norm3_g[l]), w_ple_gate[l].astype(BF16), row(b_ple_gate[l]),
                   w_ple[l].astype(BF16), row(ple_norm_g[l]),
                   row(final_g), tm=1024, n_chains=4, last_layer=l == depth - 1).reshape(b, s, d)
    return x
```

```python
import functools

import jax
import jax.numpy as jnp
from jax import lax
from jax.experimental import pallas as pl
from jax.experimental.pallas import tpu as pltpu

F32 = jnp.float32
BF16 = jnp.bfloat16

CONV_DIM = 512
CONV_WIDTH = 31
CONV_HALF = CONV_WIDTH // 2
POOL_DIM = 512
POOL_WINDOWS = (2, 4, 8, 16)
POOL_GROUP_DIM = POOL_DIM // len(POOL_WINDOWS)
N_EXPERTS = 16
CAPACITY_FACTOR = 2
EPS = 1e-6

SUBLANES = 8
HALO = 16

MIXER_TILE_ROWS = 512
MIXER_CHAINS = 2
CONV_CHUNK_ROWS = 256
TOPK_ROWS_PER_STEP = 64
FINAL_TILE_ROWS = 1024
FINAL_CHAINS = 4
V7X_VMEM_BYTES = 64 * 1024 * 1024
VMEM_LIMIT = V7X_VMEM_BYTES - 8 * 1024 * 1024
EXPERTS_VMEM_LIMIT = V7X_VMEM_BYTES - 4 * 1024 * 1024


def _rms(x, g):
    return x * lax.rsqrt(jnp.mean(x * x, axis=-1, keepdims=True) + EPS) * g


def _sigmoid(x):
    return 1.0 / (1.0 + jnp.exp(-x))


def _mixer_kernel(x_ref, xp_ref, xn_ref,
                  n1g_ref, win_ref, bg_ref, cw_ref, cb_ref, lg_ref, lb_ref, wco_ref,
                  wp_ref, ps_ref, wo_ref, n2g_ref, wrt_ref,
                  ew0_ref, ew1_ref, ew2_ref,
                  x1_ref, h2_ref, afft_ref, eb0_ref, eb1_ref, eb2_ref,
                  hext_ref, *chain_scratch, ts, seq, rc, n_chains):
    i = pl.program_id(1)
    nt = pl.num_programs(1)
    d = x_ref.shape[-1]
    c2 = 2 * CONV_DIM
    c3 = c2 + POOL_DIM
    tc = ts // n_chains
    ext = tc + 2 * HALO

    for src, dst in ((ew0_ref, eb0_ref), (ew1_ref, eb1_ref), (ew2_ref, eb2_ref)):
        dst[...] = src[...].astype(BF16)

    hext_ref[0:HALO, :] = _rms(xp_ref[...], n1g_ref[...]).astype(BF16)
    hext_ref[HALO:HALO + ts, :] = _rms(x_ref[...], n1g_ref[...]).astype(BF16)
    hext_ref[HALO + ts:2 * HALO + ts, :] = _rms(xn_ref[...], n1g_ref[...]).astype(BF16)

    def scratch(j):
        per_chain = len(chain_scratch) // n_chains
        return chain_scratch[j * per_chain:(j + 1) * per_chain]

    def in_proj(j):
        vext_ref, uext_ref, _, _ = scratch(j)
        zc = jnp.dot(hext_ref[j * tc:j * tc + ext, :], win_ref[:, :c3], preferred_element_type=F32)
        vext_ref[...] = zc[:, :CONV_DIM] * _sigmoid(zc[:, CONV_DIM:c2])
        uext_ref[...] = zc[:, c2:]
        if j == 0:
            for ref in (vext_ref, uext_ref):
                ref[0:HALO, :] = jnp.where(i == 0, 0.0, ref[0:HALO, :])
        if j == n_chains - 1:
            for ref in (vext_ref, uext_ref):
                ref[HALO + tc:ext, :] = jnp.where(i == nt - 1, 0.0, ref[HALO + tc:ext, :])

    def gate_logits(j):
        h = hext_ref[HALO + j * tc:HALO + (j + 1) * tc, :]
        return jnp.dot(h, win_ref[:, c3:], preferred_element_type=F32) + bg_ref[...]

    def conv_branch(j):
        vext_ref, _, vsh_ref, conv_ref = scratch(j)
        nsh = vsh_ref.shape[1]
        for ph in range(1, SUBLANES):
            vsh_ref[ph - 1] = vext_ref[ph:ph + nsh, :]
        for c in range(tc // rc):
            acc = jnp.zeros((rc, CONV_DIM), F32)
            for k in range(CONV_WIDTH):
                off = HALO - CONV_HALF + k
                ph = off % SUBLANES
                r0 = c * rc + off - ph
                rows = vsh_ref[ph - 1, r0:r0 + rc, :] if ph else vext_ref[r0:r0 + rc, :]
                acc = acc + rows * cw_ref[k:k + 1, :]
            conv_ref[c * rc:(c + 1) * rc, :] = acc + cb_ref[...]
        cv = conv_ref[...]
        mu = jnp.mean(cv, axis=-1, keepdims=True)
        xc = cv - mu
        var = jnp.mean(xc * xc, axis=-1, keepdims=True)
        ln = xc * lax.rsqrt(var + EPS) * lg_ref[...] + lb_ref[...]
        return (ln * _sigmoid(ln)).astype(BF16)

    def merge(j, act, gate_pre):
        _, uext_ref, _, _ = scratch(j)
        rows = slice(j * tc, (j + 1) * tc)
        y_conv = jnp.dot(act, wco_ref[...], preferred_element_type=F32)

        pos = i * ts + j * tc + lax.broadcasted_iota(jnp.int32, (tc, 1), 0)
        pooled = []
        for g, w in enumerate(POOL_WINDOWS):
            lanes = slice(g * POOL_GROUP_DIM, (g + 1) * POOL_GROUP_DIM)
            ue = uext_ref[:, lanes]
            run, width = ue, 1
            while width < w:
                run = run + pltpu.roll(run, width, axis=0)
                width *= 2
            ahead = (w - w // 2) - 1
            if ahead:
                run = pltpu.roll(run, ext - ahead, axis=0)
            lo = jnp.maximum(pos - w // 2, 0)
            hi = jnp.minimum(pos + ahead, seq - 1)
            cnt = (hi - lo + 1).astype(F32)
            dgrp = run[HALO:HALO + tc] / cnt - ue[HALO:HALO + tc]
            pooled.append(jnp.dot(dgrp.astype(BF16), wp_ref[g], preferred_element_type=F32))
        y_pool = jnp.concatenate(pooled, axis=-1) * ps_ref[...]

        gates = _sigmoid(gate_pre)
        merged = gates[:, :d] * y_conv + gates[:, d:] * y_pool
        x1 = x_ref[rows, :] + jnp.dot(merged.astype(BF16), wo_ref[...], preferred_element_type=F32)
        x1_ref[rows, :] = x1

        h2 = _rms(x1, n2g_ref[...])
        h2_ref[rows, :] = h2
        h2_hi = h2.astype(BF16)
        h2_lo = (h2 - h2_hi.astype(F32)).astype(BF16)
        logits_t = lax.dot_general(wrt_ref[...], jnp.concatenate([h2_hi, h2_lo, h2_hi], axis=-1),
                                   (((1,), (1,)), ((), ())), preferred_element_type=F32)
        ex = jnp.exp(logits_t - jnp.max(logits_t, axis=0, keepdims=True))
        afft_ref[:, rows] = ex / jnp.sum(ex, axis=0, keepdims=True)

    in_proj(0)
    pending = None
    for j in range(n_chains):
        if j + 1 < n_chains:
            in_proj(j + 1)
        gate_pre = gate_logits(j)
        act = conv_branch(j)
        if pending is not None:
            merge(*pending)
        pending = (j, act, gate_pre)
    merge(*pending)


def _mixer(x, n1g, win, bg, cw, cb, lg, lb, wco, wp, ps, wo, n2g, wrt, expert_w, ts, rc, n_chains):
    b, s, d = x.shape
    nt = s // ts
    flat_w = [w.reshape(-1, w.shape[-1]) for w in expert_w]
    share = flat_w[0].shape[0] // (b * nt)
    assert all(w.shape == flat_w[0].shape for w in flat_w) and share * b * nt == flat_w[0].shape[0]
    share_spec = pl.BlockSpec((share, flat_w[0].shape[1]), lambda bi, i: (bi * nt + i, 0))
    hb = ts // HALO
    nhb = s // HALO
    tc = ts // n_chains

    def cur(bi, i):
        return (bi, i, 0)

    def prev(bi, i):
        return (bi, jnp.maximum(i * hb - 1, 0), 0)

    def nxt(bi, i):
        return (bi, jnp.minimum((i + 1) * hb, nhb - 1), 0)

    def full(arr):
        nd = arr.ndim
        return pl.BlockSpec(arr.shape, lambda bi, i: (0,) * nd, pipeline_mode=pl.Buffered(1))

    params = (n1g, win, bg, cw, cb, lg, lb, wco, wp, ps, wo, n2g, wrt)
    chain_scratch = [
        pltpu.VMEM((tc + 2 * HALO, CONV_DIM), F32),
        pltpu.VMEM((tc + 2 * HALO, POOL_DIM), F32),
        pltpu.VMEM((SUBLANES - 1, tc + 2 * HALO - SUBLANES, CONV_DIM), F32),
        pltpu.VMEM((tc, CONV_DIM), F32),
    ]
    return pl.pallas_call(
        functools.partial(_mixer_kernel, ts=ts, seq=s, rc=rc, n_chains=n_chains),
        grid=(b, nt),
        in_specs=[
            pl.BlockSpec((None, ts, d), cur),
            pl.BlockSpec((None, HALO, d), prev),
            pl.BlockSpec((None, HALO, d), nxt),
        ] + [full(a) for a in params] + [share_spec] * 3,
        out_specs=[
            pl.BlockSpec((None, ts, d), cur),
            pl.BlockSpec((None, ts, d), cur),
            pl.BlockSpec((None, N_EXPERTS, ts), lambda bi, i: (bi, 0, i)),
        ] + [share_spec] * 3,
        out_shape=[
            jax.ShapeDtypeStruct((b, s, d), F32),
            jax.ShapeDtypeStruct((b, s, d), F32),
            jax.ShapeDtypeStruct((b, N_EXPERTS, s), F32),
        ] + [jax.ShapeDtypeStruct(w.shape, BF16) for w in flat_w],
        scratch_shapes=[pltpu.VMEM((ts + 2 * HALO, d), BF16)] + chain_scratch * n_chains,
        compiler_params=pltpu.CompilerParams(
            dimension_semantics=("parallel", "parallel"), vmem_limit_bytes=VMEM_LIMIT),
        name="mixer",
    )(x, x, x, *params, *flat_w)


def _cumsum_lanes(x):
    n = x.shape[-1]
    lane = lax.broadcasted_iota(jnp.int32, x.shape, x.ndim - 1)
    step = 1
    while step < n:
        x = x + jnp.where(lane >= step, pltpu.roll(x, step, axis=x.ndim - 1), 0)
        step *= 2
    return x


NO_TOKEN = 1 << 20


def _topk_kernel(a_ref, idx_ref, w_ref, *, cap):
    a = a_ref[...]
    as_float = lambda word: pltpu.bitcast(word, F32)
    thr = jnp.zeros((a.shape[0], 1), jnp.int32)
    for bit in range(30, -1, -1):
        cand = thr | (1 << bit)
        cnt = jnp.sum((a >= as_float(cand)).astype(jnp.int32), axis=1, keepdims=True)
        thr = jnp.where(cnt >= cap, cand, thr)
    gt = a >= as_float(thr + 1)
    eq = jnp.where(gt, 0, (a >= as_float(thr)).astype(jnp.int32))
    need = cap - jnp.sum(gt.astype(jnp.int32), axis=1, keepdims=True)
    eq_rank = _cumsum_lanes(eq) - eq
    sel = jnp.where(gt, 1, jnp.where(eq_rank < need, eq, 0))
    slot = _cumsum_lanes(sel) - sel

    s = a.shape[1]
    lane = lax.broadcasted_iota(jnp.int32, a.shape, 1)
    from_right = lambda x, k: pltpu.roll(x, s - k, axis=1)
    tok, wv = lane, a
    owed = jnp.where(sel > 0, lane - slot, NO_TOKEN)
    k = 1
    while k < s:
        in_owed = from_right(owed, k)
        incoming = (in_owed & k) != 0
        tok = jnp.where(incoming, from_right(tok, k), tok)
        wv = jnp.where(incoming, from_right(wv, k), wv)
        owed = jnp.where(incoming, in_owed ^ k, jnp.where((owed & k) != 0, NO_TOKEN, owed))
        k *= 2
    idx_ref[...] = tok[:, :cap]
    w_ref[...] = wv[:, :cap]


def _topk(afft, cap, rows_per_step):
    n, s = afft.shape
    return pl.pallas_call(
        functools.partial(_topk_kernel, cap=cap),
        grid=(n // rows_per_step,),
        in_specs=[pl.BlockSpec((rows_per_step, s), lambda i: (i, 0))],
        out_specs=[pl.BlockSpec((rows_per_step, cap), lambda i: (i, 0)),
                   pl.BlockSpec((rows_per_step, cap), lambda i: (i, 0))],
        out_shape=[jax.ShapeDtypeStruct((n, cap), jnp.int32),
                   jax.ShapeDtypeStruct((n, cap), F32)],
        compiler_params=pltpu.CompilerParams(dimension_semantics=("parallel",)),
        name="topk",
    )(afft)


WEIGHT_SLOTS = 3
SCATTER_GROUP = 4


def _experts_kernel(idx_ref, w_ref, h2_ref, wg_hbm, wu_hbm, wd_hbm, out_ref,
                    xg_a, xg_b, ye_a, ye_b, wbuf, wsem, *, cap, n_exp, n_real):
    g = pl.program_id(0)

    def weight_copies(pair):
        expert = pair % n_exp
        slot = pair % WEIGHT_SLOTS
        return [pltpu.make_async_copy(src.at[expert], wbuf.at[slot, k], wsem.at[slot, k])
                for k, src in enumerate((wg_hbm, wu_hbm, wd_hbm))]

    def gather(pair, dst):
        base = pair * cap
        for c in range(cap):
            dst[c:c + 1, :] = h2_ref[pl.ds(idx_ref[base + c], 1), :]

    @pl.when(g == 0)
    def _():
        for pair in range(WEIGHT_SLOTS - 1):
            for copy in weight_copies(pair):
                copy.start()
        ye_b[...] = jnp.zeros_like(ye_b)
        gather(0, xg_a)

    ahead = g + WEIGHT_SLOTS - 1

    @pl.when(ahead < n_real)
    def _():
        for copy in weight_copies(ahead):
            copy.start()

    @pl.when(g < n_real)
    def _():
        for copy in weight_copies(g):
            copy.wait()

    @pl.when((g == 0) | ((g - 1) % n_exp == 0))
    def _():
        out_ref[...] = jnp.zeros_like(out_ref)

    def step(xg_cur, xg_nxt, ye_cur, ye_prv):
        gather(jnp.minimum(g + 1, n_real - 1), xg_nxt)
        xg = xg_cur[...].astype(BF16)
        slot = g % WEIGHT_SLOTS
        hg = jnp.dot(xg, wbuf[slot, 0], preferred_element_type=F32)
        hu = jnp.dot(xg, wbuf[slot, 1], preferred_element_type=F32)
        hid = (hg * _sigmoid(hg) * hu).astype(BF16)
        ye_cur[...] = jnp.dot(hid, wbuf[slot, 2], preferred_element_type=F32)
        base = jnp.maximum(g - 1, 0) * cap
        for c0 in range(0, cap, SCATTER_GROUP):
            toks = [idx_ref[base + c0 + i] for i in range(SCATTER_GROUP)]
            rows = [out_ref[pl.ds(toks[i], 1), :] + ye_prv[c0 + i:c0 + i + 1, :] * w_ref[base + c0 + i]
                    for i in range(SCATTER_GROUP)]
            for i in range(SCATTER_GROUP):
                out_ref[pl.ds(toks[i], 1), :] = rows[i]

    @pl.when(g % 2 == 0)
    def _():
        step(xg_a, xg_b, ye_a, ye_b)

    @pl.when(g % 2 == 1)
    def _():
        step(xg_b, xg_a, ye_b, ye_a)


def _experts(h2, idx, w, wg, wu, wd, cap):
    b, s, d = h2.shape
    ne, _, f = wg.shape
    n_real = b * ne
    last = n_real - 1
    seq_of = lambda pair: pair // ne
    assert wg.shape == wu.shape == (ne, d, f) and wd.shape == (ne, f, d) and d == f
    return pl.pallas_call(
        functools.partial(_experts_kernel, cap=cap, n_exp=ne, n_real=n_real),
        grid_spec=pltpu.PrefetchScalarGridSpec(
            num_scalar_prefetch=2,
            grid=(n_real + 1,),
            in_specs=[
                pl.BlockSpec((None, s, d), lambda g, idx_, w_: (seq_of(jnp.minimum(g + 1, last)), 0, 0)),
            ] + [pl.BlockSpec(memory_space=pl.ANY)] * 3,
            out_specs=pl.BlockSpec((None, s, d), lambda g, idx_, w_: (seq_of(jnp.maximum(g - 1, 0)), 0, 0)),
            scratch_shapes=[pltpu.VMEM((cap, d), F32)] * 4 + [
                pltpu.VMEM((WEIGHT_SLOTS, 3, d, f), BF16),
                pltpu.SemaphoreType.DMA((WEIGHT_SLOTS, 3)),
            ],
        ),
        out_shape=jax.ShapeDtypeStruct((b, s, d), F32),
        compiler_params=pltpu.CompilerParams(
            dimension_semantics=("arbitrary",), vmem_limit_bytes=EXPERTS_VMEM_LIMIT),
        name="experts",
    )(idx.reshape(-1), w.reshape(-1), h2, wg, wu, wd)


def _final_kernel(x1_ref, moe_ref, p_ref, n3g_ref, wpg_ref, bpg_ref, wple_ref, pg_ref, fg_ref, o_ref,
                  *, last_layer, n_chains):
    tc = o_ref.shape[0] // n_chains
    for j in range(n_chains):
        rows = slice(j * tc, (j + 1) * tc)
        x2 = x1_ref[rows, :] + moe_ref[rows, :]
        hn = _rms(x2, n3g_ref[...]).astype(BF16)
        gate = _sigmoid(jnp.dot(hn, wpg_ref[...], preferred_element_type=F32) + bpg_ref[...])
        emb = jnp.dot(p_ref[rows, :].astype(BF16), wple_ref[...], preferred_element_type=F32)
        x3 = x2 + gate * _rms(emb, pg_ref[...])
        o_ref[rows, :] = _rms(x3, fg_ref[...]) if last_layer else x3


def _final(x1, moe, p2d, n3g, wpg, bpg, wple, pg, fg, tm, n_chains, last_layer):
    t, d = x1.shape
    q = p2d.shape[1]

    def full(arr):
        nd = arr.ndim
        return pl.BlockSpec(arr.shape, lambda i: (0,) * nd)

    params = (n3g, wpg, bpg, wple, pg, fg)
    return pl.pallas_call(
        functools.partial(_final_kernel, last_layer=last_layer, n_chains=n_chains),
        grid=(t // tm,),
        in_specs=[
            pl.BlockSpec((tm, d), lambda i: (i, 0)),
            pl.BlockSpec((tm, d), lambda i: (i, 0)),
            pl.BlockSpec((tm, q), lambda i: (i, 0)),
        ] + [full(a) for a in params],
        out_specs=pl.BlockSpec((tm, d), lambda i: (i, 0)),
        out_shape=jax.ShapeDtypeStruct((t, d), F32),
        compiler_params=pltpu.CompilerParams(
            dimension_semantics=("parallel",), vmem_limit_bytes=VMEM_LIMIT),
        name="final",
    )(x1, moe, p2d, *params)


def _router_pieces(w_router):
    w_hi = w_router.astype(BF16)
    w_lo = (w_router - w_hi.astype(F32)).astype(BF16)
    return jnp.concatenate([w_hi, w_hi, w_lo], axis=0).T


def kernel(x, p, norm1_g, w_in, b_gate, conv_w, conv_b, conv_ln_g, conv_ln_b, w_conv_out, w_pool,
           pool_scale, w_out, norm2_g, w_router, w_exp_gate, w_exp_up, w_exp_down, norm3_g,
           w_ple_gate, b_ple_gate, w_ple, ple_norm_g, final_g):
    b, s, d = x.shape
    depth = w_in.shape[0]
    cap = max(1, CAPACITY_FACTOR * s // N_EXPERTS)
    row = lambda a: a.reshape(1, -1)

    for l in range(depth):
        expert_w = (w_exp_gate[l], w_exp_up[l], w_exp_down[l])
        x1, h2, afft, *expert_b = _mixer(
            x, row(norm1_g[l]), w_in[l].astype(BF16), row(b_gate[l]), conv_w[l], row(conv_b[l]),
            row(conv_ln_g[l]), row(conv_ln_b[l]), w_conv_out[l].astype(BF16),
            w_pool[l].astype(BF16), row(pool_scale[l]), w_out[l].astype(BF16),
            row(norm2_g[l]), _router_pieces(w_router[l]), expert_w,
            ts=MIXER_TILE_ROWS, rc=CONV_CHUNK_ROWS, n_chains=MIXER_CHAINS)
        idx, wts = _topk(afft.reshape(b * N_EXPERTS, s), cap, rows_per_step=TOPK_ROWS_PER_STEP)
        moe = _experts(h2, idx, wts, *(wb.reshape(w.shape) for wb, w in zip(expert_b, expert_w)), cap)
        x = _final(x1.reshape(b * s, d), moe.reshape(b * s, d), p[l].reshape(b * s, -1),
                   row(norm3_g[l]), w_ple_gate[l].astype(BF16), row(b_ple_gate[l]),
                   w_ple[l].astype(BF16), row(ple_norm_g[l]),
                   row(final_g), tm=FINAL_TILE_ROWS, n_chains=FINAL_CHAINS, last_layer=l == depth - 1).reshape(b, s, d)
    return x
```

```python
import functools

import jax
import jax.numpy as jnp
from jax import lax
from jax.experimental import pallas as pl
from jax.experimental.pallas import tpu as pltpu

F32 = jnp.float32
BF16 = jnp.bfloat16

CONV_DIM = 512
CONV_WIDTH = 31
CONV_HALF = CONV_WIDTH // 2
POOL_DIM = 512
POOL_WINDOWS = (2, 4, 8, 16)
POOL_GROUP_DIM = POOL_DIM // len(POOL_WINDOWS)
N_EXPERTS = 16
CAPACITY_FACTOR = 2
EPS = 1e-6

SUBLANES = 8
HALO = 16

MIXER_TILE_ROWS = 512
MIXER_CHAINS = 2
CONV_CHUNK_ROWS = 256
TOPK_ROWS_PER_STEP = 64
FINAL_TILE_ROWS = 1024
FINAL_CHAINS = 4
V7X_VMEM_BYTES = 64 * 1024 * 1024
VMEM_LIMIT = V7X_VMEM_BYTES - 8 * 1024 * 1024
EXPERTS_VMEM_LIMIT = V7X_VMEM_BYTES - 4 * 1024 * 1024


def _rms(x, g):
    return x * lax.rsqrt(jnp.mean(x * x, axis=-1, keepdims=True) + EPS) * g


def _sigmoid(x):
    return 1.0 / (1.0 + jnp.exp(-x))


def _mixer_kernel(x_ref, xp_ref, xn_ref,
                  n1g_ref, win_ref, bg_ref, cw_ref, cb_ref, lg_ref, lb_ref, wco_ref,
                  wp_ref, ps_ref, wo_ref, n2g_ref, wrt_ref,
                  ew0_ref, ew1_ref, ew2_ref,
                  x1_ref, h2_ref, afft_ref, eb0_ref, eb1_ref, eb2_ref,
                  hext_ref, *chain_scratch, ts, seq, rc, n_chains):
    i = pl.program_id(1)
    nt = pl.num_programs(1)
    d = x_ref.shape[-1]
    c2 = 2 * CONV_DIM
    c3 = c2 + POOL_DIM
    tc = ts // n_chains
    ext = tc + 2 * HALO

    for src, dst in ((ew0_ref, eb0_ref), (ew1_ref, eb1_ref), (ew2_ref, eb2_ref)):
        dst[...] = src[...].astype(BF16)

    hext_ref[0:HALO, :] = _rms(xp_ref[...], n1g_ref[...]).astype(BF16)
    hext_ref[HALO:HALO + ts, :] = _rms(x_ref[...], n1g_ref[...]).astype(BF16)
    hext_ref[HALO + ts:2 * HALO + ts, :] = _rms(xn_ref[...], n1g_ref[...]).astype(BF16)

    def scratch(j):
        per_chain = len(chain_scratch) // n_chains
        return chain_scratch[j * per_chain:(j + 1) * per_chain]

    def in_proj(j):
        vext_ref, uext_ref, _, _ = scratch(j)
        zc = jnp.dot(hext_ref[j * tc:j * tc + ext, :], win_ref[:, :c3], preferred_element_type=F32)
        vext_ref[...] = zc[:, :CONV_DIM] * _sigmoid(zc[:, CONV_DIM:c2])
        uext_ref[...] = zc[:, c2:]
        if j == 0:
            for ref in (vext_ref, uext_ref):
                ref[0:HALO, :] = jnp.where(i == 0, 0.0, ref[0:HALO, :])
        if j == n_chains - 1:
            for ref in (vext_ref, uext_ref):
                ref[HALO + tc:ext, :] = jnp.where(i == nt - 1, 0.0, ref[HALO + tc:ext, :])

    def gate_logits(j):
        h = hext_ref[HALO + j * tc:HALO + (j + 1) * tc, :]
        return jnp.dot(h, win_ref[:, c3:], preferred_element_type=F32) + bg_ref[...]

    def conv_branch(j):
        vext_ref, _, vsh_ref, conv_ref = scratch(j)
        nsh = vsh_ref.shape[1]
        for ph in range(1, SUBLANES):
            vsh_ref[ph - 1] = vext_ref[ph:ph + nsh, :]
        for c in range(tc // rc):
            acc = jnp.zeros((rc, CONV_DIM), F32)
            for k in range(CONV_WIDTH):
                off = HALO - CONV_HALF + k
                ph = off % SUBLANES
                r0 = c * rc + off - ph
                rows = vsh_ref[ph - 1, r0:r0 + rc, :] if ph else vext_ref[r0:r0 + rc, :]
                acc = acc + rows * cw_ref[k:k + 1, :]
            conv_ref[c * rc:(c + 1) * rc, :] = acc + cb_ref[...]
        cv = conv_ref[...]
        mu = jnp.mean(cv, axis=-1, keepdims=True)
        xc = cv - mu
        var = jnp.mean(xc * xc, axis=-1, keepdims=True)
        ln = xc * lax.rsqrt(var + EPS) * lg_ref[...] + lb_ref[...]
        return (ln * _sigmoid(ln)).astype(BF16)

    def merge(j, act, gate_pre):
        _, uext_ref, _, _ = scratch(j)
        rows = slice(j * tc, (j + 1) * tc)
        y_conv = jnp.dot(act, wco_ref[...], preferred_element_type=F32)

        pos = i * ts + j * tc + lax.broadcasted_iota(jnp.int32, (tc, 1), 0)
        pooled = []
        for g, w in enumerate(POOL_WINDOWS):
            lanes = slice(g * POOL_GROUP_DIM, (g + 1) * POOL_GROUP_DIM)
            ue = uext_ref[:, lanes]
            run, width = ue, 1
            while width < w:
                run = run + pltpu.roll(run, width, axis=0)
                width *= 2
            ahead = (w - w // 2) - 1
            if ahead:
                run = pltpu.roll(run, ext - ahead, axis=0)
            lo = jnp.maximum(pos - w // 2, 0)
            hi = jnp.minimum(pos + ahead, seq - 1)
            cnt = (hi - lo + 1).astype(F32)
            dgrp = run[HALO:HALO + tc] / cnt - ue[HALO:HALO + tc]
            pooled.append(jnp.dot(dgrp.astype(BF16), wp_ref[g], preferred_element_type=F32))
        y_pool = jnp.concatenate(pooled, axis=-1) * ps_ref[...]

        gates = _sigmoid(gate_pre)
        merged = gates[:, :d] * y_conv + gates[:, d:] * y_pool
        x1 = x_ref[rows, :] + jnp.dot(merged.astype(BF16), wo_ref[...], preferred_element_type=F32)
        x1_ref[rows, :] = x1

        h2 = _rms(x1, n2g_ref[...])
        h2_ref[rows, :] = h2
        h2_hi = h2.astype(BF16)
        h2_lo = (h2 - h2_hi.astype(F32)).astype(BF16)
        logits_t = lax.dot_general(wrt_ref[...], jnp.concatenate([h2_hi, h2_lo, h2_hi], axis=-1),
                                   (((1,), (1,)), ((), ())), preferred_element_type=F32)
        ex = jnp.exp(logits_t - jnp.max(logits_t, axis=0, keepdims=True))
        afft_ref[:, rows] = ex / jnp.sum(ex, axis=0, keepdims=True)

    in_proj(0)
    pending = None
    for j in range(n_chains):
        if j + 1 < n_chains:
            in_proj(j + 1)
        gate_pre = gate_logits(j)
        act = conv_branch(j)
        if pending is not None:
            merge(*pending)
        pending = (j, act, gate_pre)
    merge(*pending)


def _mixer(x, n1g, win, bg, cw, cb, lg, lb, wco, wp, ps, wo, n2g, wrt, expert_w, ts, rc, n_chains):
    b, s, d = x.shape
    nt = s // ts
    flat_w = [w.reshape(-1, w.shape[-1]) for w in expert_w]
    share = flat_w[0].shape[0] // (b * nt)
    assert all(w.shape == flat_w[0].shape for w in flat_w) and share * b * nt == flat_w[0].shape[0]
    share_spec = pl.BlockSpec((share, flat_w[0].shape[1]), lambda bi, i: (bi * nt + i, 0))
    hb = ts // HALO
    nhb = s // HALO
    tc = ts // n_chains

    def cur(bi, i):
        return (bi, i, 0)

    def prev(bi, i):
        return (bi, jnp.maximum(i * hb - 1, 0), 0)

    def nxt(bi, i):
        return (bi, jnp.minimum((i + 1) * hb, nhb - 1), 0)

    def full(arr):
        nd = arr.ndim
        return pl.BlockSpec(arr.shape, lambda bi, i: (0,) * nd, pipeline_mode=pl.Buffered(1))

    params = (n1g, win, bg, cw, cb, lg, lb, wco, wp, ps, wo, n2g, wrt)
    chain_scratch = [
        pltpu.VMEM((tc + 2 * HALO, CONV_DIM), F32),
        pltpu.VMEM((tc + 2 * HALO, POOL_DIM), F32),
        pltpu.VMEM((SUBLANES - 1, tc + 2 * HALO - SUBLANES, CONV_DIM), F32),
        pltpu.VMEM((tc, CONV_DIM), F32),
    ]
    return pl.pallas_call(
        functools.partial(_mixer_kernel, ts=ts, seq=s, rc=rc, n_chains=n_chains),
        grid=(b, nt),
        in_specs=[
            pl.BlockSpec((None, ts, d), cur),
            pl.BlockSpec((None, HALO, d), prev),
            pl.BlockSpec((None, HALO, d), nxt),
        ] + [full(a) for a in params] + [share_spec] * 3,
        out_specs=[
            pl.BlockSpec((None, ts, d), cur),
            pl.BlockSpec((None, ts, d), cur),
            pl.BlockSpec((None, N_EXPERTS, ts), lambda bi, i: (bi, 0, i)),
        ] + [share_spec] * 3,
        out_shape=[
            jax.ShapeDtypeStruct((b, s, d), F32),
            jax.ShapeDtypeStruct((b, s, d), F32),
            jax.ShapeDtypeStruct((b, N_EXPERTS, s), F32),
        ] + [jax.ShapeDtypeStruct(w.shape, BF16) for w in flat_w],
        scratch_shapes=[pltpu.VMEM((ts + 2 * HALO, d), BF16)] + chain_scratch * n_chains,
        compiler_params=pltpu.CompilerParams(
            dimension_semantics=("parallel", "parallel"), vmem_limit_bytes=VMEM_LIMIT),
        name="mixer",
    )(x, x, x, *params, *flat_w)


def _cumsum_lanes(x):
    n = x.shape[-1]
    lane = lax.broadcasted_iota(jnp.int32, x.shape, x.ndim - 1)
    step = 1
    while step < n:
        x = x + jnp.where(lane >= step, pltpu.roll(x, step, axis=x.ndim - 1), 0)
        step *= 2
    return x


NO_TOKEN = 1 << 20


def _topk_kernel(a_ref, idx_ref, w_ref, *, cap):
    a = a_ref[...]
    as_float = lambda word: pltpu.bitcast(word, F32)
    thr = jnp.zeros((a.shape[0], 1), jnp.int32)
    for bit in range(30, -1, -1):
        cand = thr | (1 << bit)
        cnt = jnp.sum((a >= as_float(cand)).astype(jnp.int32), axis=1, keepdims=True)
        thr = jnp.where(cnt >= cap, cand, thr)
    gt = a >= as_float(thr + 1)
    eq = jnp.where(gt, 0, (a >= as_float(thr)).astype(jnp.int32))
    need = cap - jnp.sum(gt.astype(jnp.int32), axis=1, keepdims=True)
    eq_rank = _cumsum_lanes(eq) - eq
    sel = jnp.where(gt, 1, jnp.where(eq_rank < need, eq, 0))
    slot = _cumsum_lanes(sel) - sel

    s = a.shape[1]
    lane = lax.broadcasted_iota(jnp.int32, a.shape, 1)
    from_right = lambda x, k: pltpu.roll(x, s - k, axis=1)
    tok, wv = lane, a
    owed = jnp.where(sel > 0, lane - slot, NO_TOKEN)
    k = 1
    while k < s:
        in_owed = from_right(owed, k)
        incoming = (in_owed & k) != 0
        tok = jnp.where(incoming, from_right(tok, k), tok)
        wv = jnp.where(incoming, from_right(wv, k), wv)
        owed = jnp.where(incoming, in_owed ^ k, jnp.where((owed & k) != 0, NO_TOKEN, owed))
        k *= 2
    idx_ref[...] = tok[:, :cap]
    w_ref[...] = wv[:, :cap]


def _topk(afft, cap, rows_per_step):
    n, s = afft.shape
    return pl.pallas_call(
        functools.partial(_topk_kernel, cap=cap),
        grid=(n // rows_per_step,),
        in_specs=[pl.BlockSpec((rows_per_step, s), lambda i: (i, 0))],
        out_specs=[pl.BlockSpec((rows_per_step, cap), lambda i: (i, 0)),
                   pl.BlockSpec((rows_per_step, cap), lambda i: (i, 0))],
        out_shape=[jax.ShapeDtypeStruct((n, cap), jnp.int32),
                   jax.ShapeDtypeStruct((n, cap), F32)],
        compiler_params=pltpu.CompilerParams(dimension_semantics=("parallel",)),
        name="topk",
    )(afft)


WEIGHT_SLOTS = 3
FFN_ROW_PARTS = 2
SCATTER_GROUP = 4


def _experts_kernel(idx_ref, w_ref, h2_ref, wg_hbm, wu_hbm, wd_hbm, out_ref,
                    xg_a, xg_b, ye_a, ye_b, wbuf, wsem, *, cap, n_exp, n_real):
    g = pl.program_id(0)

    def weight_copies(pair):
        expert = pair % n_exp
        slot = pair % WEIGHT_SLOTS
        return [pltpu.make_async_copy(src.at[expert], wbuf.at[slot, k], wsem.at[slot, k])
                for k, src in enumerate((wg_hbm, wu_hbm, wd_hbm))]

    def gather(pair, dst):
        base = pair * cap
        for c in range(cap):
            dst[c:c + 1, :] = h2_ref[pl.ds(idx_ref[base + c], 1), :]

    @pl.when(g == 0)
    def _():
        for pair in range(WEIGHT_SLOTS - 1):
            for copy in weight_copies(pair):
                copy.start()
        ye_b[...] = jnp.zeros_like(ye_b)
        gather(0, xg_a)

    ahead = g + WEIGHT_SLOTS - 1

    @pl.when(ahead < n_real)
    def _():
        for copy in weight_copies(ahead):
            copy.start()

    @pl.when(g < n_real)
    def _():
        for copy in weight_copies(g):
            copy.wait()

    @pl.when((g == 0) | ((g - 1) % n_exp == 0))
    def _():
        out_ref[...] = jnp.zeros_like(out_ref)

    def step(xg_cur, xg_nxt, ye_cur, ye_prv):
        gather(jnp.minimum(g + 1, n_real - 1), xg_nxt)
        slot = g % WEIGHT_SLOTS
        for part in range(FFN_ROW_PARTS):
            rows = slice(part * cap // FFN_ROW_PARTS, (part + 1) * cap // FFN_ROW_PARTS)
            xg = xg_cur[rows, :].astype(BF16)
            hg = jnp.dot(xg, wbuf[slot, 0], preferred_element_type=F32)
            hu = jnp.dot(xg, wbuf[slot, 1], preferred_element_type=F32)
            hid = (hg * _sigmoid(hg) * hu).astype(BF16)
            ye_cur[rows, :] = jnp.dot(hid, wbuf[slot, 2], preferred_element_type=F32)
        base = jnp.maximum(g - 1, 0) * cap
        for c0 in range(0, cap, SCATTER_GROUP):
            toks = [idx_ref[base + c0 + i] for i in range(SCATTER_GROUP)]
            rows = [out_ref[pl.ds(toks[i], 1), :] + ye_prv[c0 + i:c0 + i + 1, :] * w_ref[base + c0 + i]
                    for i in range(SCATTER_GROUP)]
            for i in range(SCATTER_GROUP):
                out_ref[pl.ds(toks[i], 1), :] = rows[i]

    @pl.when(g % 2 == 0)
    def _():
        step(xg_a, xg_b, ye_a, ye_b)

    @pl.when(g % 2 == 1)
    def _():
        step(xg_b, xg_a, ye_b, ye_a)


def _experts(h2, idx, w, wg, wu, wd, cap):
    b, s, d = h2.shape
    ne, _, f = wg.shape
    n_real = b * ne
    last = n_real - 1
    seq_of = lambda pair: pair // ne
    assert wg.shape == wu.shape == (ne, d, f) and wd.shape == (ne, f, d) and d == f
    return pl.pallas_call(
        functools.partial(_experts_kernel, cap=cap, n_exp=ne, n_real=n_real),
        grid_spec=pltpu.PrefetchScalarGridSpec(
            num_scalar_prefetch=2,
            grid=(n_real + 1,),
            in_specs=[
                pl.BlockSpec((None, s, d), lambda g, idx_, w_: (seq_of(jnp.minimum(g + 1, last)), 0, 0)),
            ] + [pl.BlockSpec(memory_space=pl.ANY)] * 3,
            out_specs=pl.BlockSpec((None, s, d), lambda g, idx_, w_: (seq_of(jnp.maximum(g - 1, 0)), 0, 0)),
            scratch_shapes=[pltpu.VMEM((cap, d), F32)] * 4 + [
                pltpu.VMEM((WEIGHT_SLOTS, 3, d, f), BF16),
                pltpu.SemaphoreType.DMA((WEIGHT_SLOTS, 3)),
            ],
        ),
        out_shape=jax.ShapeDtypeStruct((b, s, d), F32),
        compiler_params=pltpu.CompilerParams(
            dimension_semantics=("arbitrary",), vmem_limit_bytes=EXPERTS_VMEM_LIMIT),
        name="experts",
    )(idx.reshape(-1), w.reshape(-1), h2, wg, wu, wd)


def _final_kernel(x1_ref, moe_ref, p_ref, n3g_ref, wpg_ref, bpg_ref, wple_ref, pg_ref, fg_ref, o_ref,
                  *, last_layer, n_chains):
    tc = o_ref.shape[0] // n_chains
    for j in range(n_chains):
        rows = slice(j * tc, (j + 1) * tc)
        x2 = x1_ref[rows, :] + moe_ref[rows, :]
        hn = _rms(x2, n3g_ref[...]).astype(BF16)
        gate = _sigmoid(jnp.dot(hn, wpg_ref[...], preferred_element_type=F32) + bpg_ref[...])
        emb = jnp.dot(p_ref[rows, :].astype(BF16), wple_ref[...], preferred_element_type=F32)
        x3 = x2 + gate * _rms(emb, pg_ref[...])
        o_ref[rows, :] = _rms(x3, fg_ref[...]) if last_layer else x3


def _final(x1, moe, p2d, n3g, wpg, bpg, wple, pg, fg, tm, n_chains, last_layer):
    t, d = x1.shape
    q = p2d.shape[1]

    def full(arr):
        nd = arr.ndim
        return pl.BlockSpec(arr.shape, lambda i: (0,) * nd)

    params = (n3g, wpg, bpg, wple, pg, fg)
    return pl.pallas_call(
        functools.partial(_final_kernel, last_layer=last_layer, n_chains=n_chains),
        grid=(t // tm,),
        in_specs=[
            pl.BlockSpec((tm, d), lambda i: (i, 0)),
            pl.BlockSpec((tm, d), lambda i: (i, 0)),
            pl.BlockSpec((tm, q), lambda i: (i, 0)),
        ] + [full(a) for a in params],
        out_specs=pl.BlockSpec((tm, d), lambda i: (i, 0)),
        out_shape=jax.ShapeDtypeStruct((t, d), F32),
        compiler_params=pltpu.CompilerParams(
            dimension_semantics=("parallel",), vmem_limit_bytes=VMEM_LIMIT),
        name="final",
    )(x1, moe, p2d, *params)


def _router_pieces(w_router):
    w_hi = w_router.astype(BF16)
    w_lo = (w_router - w_hi.astype(F32)).astype(BF16)
    return jnp.concatenate([w_hi, w_hi, w_lo], axis=0).T


def kernel(x, p, norm1_g, w_in, b_gate, conv_w, conv_b, conv_ln_g, conv_ln_b, w_conv_out, w_pool,
           pool_scale, w_out, norm2_g, w_router, w_exp_gate, w_exp_up, w_exp_down, norm3_g,
           w_ple_gate, b_ple_gate, w_ple, ple_norm_g, final_g):
    b, s, d = x.shape
    depth = w_in.shape[0]
    cap = max(1, CAPACITY_FACTOR * s // N_EXPERTS)
    row = lambda a: a.reshape(1, -1)

    for l in range(depth):
        expert_w = (w_exp_gate[l], w_exp_up[l], w_exp_down[l])
        x1, h2, afft, *expert_b = _mixer(
            x, row(norm1_g[l]), w_in[l].astype(BF16), row(b_gate[l]), conv_w[l], row(conv_b[l]),
            row(conv_ln_g[l]), row(conv_ln_b[l]), w_conv_out[l].astype(BF16),
            w_pool[l].astype(BF16), row(pool_scale[l]), w_out[l].astype(BF16),
            row(norm2_g[l]), _router_pieces(w_router[l]), expert_w,
            ts=MIXER_TILE_ROWS, rc=CONV_CHUNK_ROWS, n_chains=MIXER_CHAINS)
        idx, wts = _topk(afft.reshape(b * N_EXPERTS, s), cap, rows_per_step=TOPK_ROWS_PER_STEP)
        moe = _experts(h2, idx, wts, *(wb.reshape(w.shape) for wb, w in zip(expert_b, expert_w)), cap)
        x = _final(x1.reshape(b * s, d), moe.reshape(b * s, d), p[l].reshape(b * s, -1),
                   row(norm3_g[l]), w_ple_gate[l].astype(BF16), row(b_ple_gate[l]),
                   w_ple[l].astype(BF16), row(ple_norm_g[l]),
                   row(final_g), tm=FINAL_TILE_ROWS, n_chains=FINAL_CHAINS, last_layer=l == depth - 1).reshape(b, s, d)
    return x
```
